```python
import jax, jax.numpy as jnp
from jax import lax
import numpy as np

D_MODEL = 2048
BATCH = 4
SEQ = 2048
DEPTH = 1
DEC_BATCH = 128
DEC_SEQ = 4
PAST_LEN = 16384
PAGE_SIZE = 128

D_CONV = D_MODEL
CONV_WIDTH = 3
POOL_WINDOWS = (2, 4, 8, 16)
N_POOL_GROUPS = len(POOL_WINDOWS)
D_POOL = D_MODEL // 2
D_POOL_GROUP = D_POOL // N_POOL_GROUPS
POOL_BUF = max(POOL_WINDOWS) - 1
D_FF = ((8 * D_MODEL + 3 * 256 - 1) // (3 * 256)) * 256
D_IN = 3 * D_CONV + D_POOL + 2 * D_MODEL
EPS = 1e-6

kernel_name = "gated_shortconv_multiscale_pool_hybrid_step"


def rmsnorm(x, g):
    xf = x.astype(jnp.float32)
    r = lax.rsqrt(jnp.mean(xf * xf, axis=-1, keepdims=True) + EPS)
    return (xf * r).astype(x.dtype) * g


def pool_counts(pos0, T):
    t = jnp.arange(T, dtype=jnp.int32) + pos0
    w = jnp.array(POOL_WINDOWS, dtype=jnp.int32)
    return jnp.minimum(t[:, None] + 1, w[None, :]).astype(jnp.float32)


def causal_multiscale_pool(v_ext, counts):
    T = v_ext.shape[1] - POOL_BUF
    vf = v_ext.astype(jnp.float32)
    cs = jnp.cumsum(vf, axis=1)
    cs0 = jnp.concatenate([jnp.zeros_like(cs[:, :1]), cs], axis=1)
    end = cs0[:, POOL_BUF + 1:]
    outs = []
    for g, w in enumerate(POOL_WINDOWS):
        lo, hi = g * D_POOL_GROUP, (g + 1) * D_POOL_GROUP
        start = cs0[:, POOL_BUF + 1 - w: POOL_BUF + 1 - w + T, lo:hi]
        outs.append((end[..., lo:hi] - start) / counts[None, :, g:g + 1])
    mean = jnp.concatenate(outs, axis=-1)
    return (mean - vf[:, POOL_BUF:]).astype(v_ext.dtype)


def decoder_layer(x, conv_buf, pool_buf, counts, norm_mix, w_in, conv_w, w_pool, pool_scale,
                  w_br_conv, w_br_pool, w_o, norm_ffn, w_gate, w_up, w_down):
    Bsz, T, _ = x.shape
    xn = rmsnorm(x, norm_mix)
    proj = jnp.einsum('btd,de->bte', xn, w_in)
    h, b, c, v, gc, gp = jnp.split(
        proj, [D_CONV, 2 * D_CONV, 3 * D_CONV, 3 * D_CONV + D_POOL, 3 * D_CONV + D_POOL + D_MODEL], axis=-1)
    u_ext = jnp.concatenate([conv_buf, c * h], axis=1)
    conv = conv_w[0] * u_ext[:, 0:T] + conv_w[1] * u_ext[:, 1:T + 1] + conv_w[2] * u_ext[:, 2:T + 2]
    y_conv = jnp.einsum('btc,cd->btd', b * conv, w_br_conv)
    v_ext = jnp.concatenate([pool_buf, v], axis=1)
    pooled = causal_multiscale_pool(v_ext, counts).reshape(Bsz, T, N_POOL_GROUPS, D_POOL_GROUP)
    mixed = jnp.einsum('btgc,gcd->btgd', pooled, w_pool).reshape(Bsz, T, D_POOL) * pool_scale
    y_pool = jnp.einsum('btc,cd->btd', mixed, w_br_pool)
    merged = jax.nn.sigmoid(gc) * y_conv + jax.nn.sigmoid(gp) * y_pool
    x = x + jnp.einsum('btd,de->bte', merged, w_o)
    hn = rmsnorm(x, norm_ffn)
    ff = jax.nn.silu(jnp.einsum('btd,df->btf', hn, w_gate)) * jnp.einsum('btd,df->btf', hn, w_up)
    x = x + jnp.einsum('btf,fd->btd', ff, w_down)
    return x, u_ext[:, -(CONV_WIDTH - 1):], v_ext[:, -POOL_BUF:]


def setup_inputs(seed: int = 0) -> dict:
    key = jax.random.key(seed)
    ks = jax.random.split(key, 20)
    f32 = jnp.float32
    nrm = lambda k, s, sc: jax.random.normal(k, s, f32) * sc
    return {
        "x_prompt": nrm(ks[0], (BATCH, SEQ, D_MODEL), 1.0),
        "x_sample": nrm(ks[1], (DEC_BATCH, DEC_SEQ, D_MODEL), 1.0),
        "state_conv": nrm(ks[2], (DEPTH, DEC_BATCH, CONV_WIDTH - 1, D_CONV), 1.0),
        "state_pool": nrm(ks[3], (DEPTH, DEC_BATCH, POOL_BUF, D_POOL), 1.0),
        "norm_mix": 1.0 + nrm(ks[4], (DEPTH, D_MODEL), 0.1),
        "w_in": nrm(ks[5], (DEPTH, D_MODEL, D_IN), D_MODEL ** -0.5),
        "conv_w": nrm(ks[6], (DEPTH, CONV_WIDTH, D_CONV), CONV_WIDTH ** -0.5),
        "w_pool": nrm(ks[7], (DEPTH, N_POOL_GROUPS, D_POOL_GROUP, D_POOL_GROUP), D_POOL_GROUP ** -0.5),
        "pool_scale": 1.0 + nrm(ks[8], (DEPTH, D_POOL), 0.1),
        "w_br_conv": nrm(ks[9], (DEPTH, D_CONV, D_MODEL), D_CONV ** -0.5),
        "w_br_pool": nrm(ks[10], (DEPTH, D_POOL, D_MODEL), D_POOL ** -0.5),
        "w_o": nrm(ks[11], (DEPTH, D_MODEL, D_MODEL), D_MODEL ** -0.5),
        "norm_ffn": 1.0 + nrm(ks[12], (DEPTH, D_MODEL), 0.1),
        "w_gate": nrm(ks[13], (DEPTH, D_MODEL, D_FF), D_MODEL ** -0.5),
        "w_up": nrm(ks[14], (DEPTH, D_MODEL, D_FF), D_MODEL ** -0.5),
        "w_down": nrm(ks[15], (DEPTH, D_FF, D_MODEL), D_FF ** -0.5),
        "norm_final": 1.0 + nrm(ks[16], (D_MODEL,), 0.1),
    }


def reference(x_prompt, x_sample, state_conv, state_pool, norm_mix, w_in, conv_w, w_pool, pool_scale,
              w_br_conv, w_br_pool, w_o, norm_ffn, w_gate, w_up, w_down, norm_final):
    Bp, Tp, _ = x_prompt.shape
    Ts = x_sample.shape[1]
    counts_p = pool_counts(0, Tp)
    counts_s = pool_counts(PAST_LEN, Ts)
    yp, ys = x_prompt, x_sample
    conv_p, pool_p, conv_s, pool_s = [], [], [], []
    for l in range(DEPTH):
        params = (norm_mix[l], w_in[l], conv_w[l], w_pool[l], pool_scale[l], w_br_conv[l], w_br_pool[l],
                  w_o[l], norm_ffn[l], w_gate[l], w_up[l], w_down[l])
        zc = jnp.zeros((Bp, CONV_WIDTH - 1, D_CONV), x_prompt.dtype)
        zp = jnp.zeros((Bp, POOL_BUF, D_POOL), x_prompt.dtype)
        yp, cp, pp = decoder_layer(yp, zc, zp, counts_p, *params)
        ys, cs_, ps_ = decoder_layer(ys, state_conv[l], state_pool[l], counts_s, *params)
        conv_p.append(cp); pool_p.append(pp); conv_s.append(cs_); pool_s.append(ps_)
    y_prompt = rmsnorm(yp, norm_final)
    y_sample = rmsnorm(ys, norm_final)
    new_conv_prompt = jnp.stack(conv_p, axis=0)
    new_pool_prompt = jnp.stack(pool_p, axis=0)
    new_conv_sample = jnp.stack(conv_s, axis=0)
    new_pool_sample = jnp.stack(pool_s, axis=0)
    return (y_prompt, y_sample, new_conv_prompt, new_pool_prompt, new_conv_sample, new_pool_sample)
```

```python
import functools

import jax
import jax.numpy as jnp
from jax import lax
from jax.experimental import pallas as pl
from jax.experimental.pallas import tpu as pltpu

EPS = 1e-6
CONV_WIDTH = 3
POOL_WINDOWS = (2, 4, 8, 16)
POOL_BUF = max(POOL_WINDOWS) - 1
PAST_LEN = 16384

SUBLANES = 8
HEAD_ROWS = 16
POOL_CARRY_ROWS = 16
VMEM_LIMIT_BYTES = 56 * 1024 * 1024

BF16 = jnp.bfloat16
F32 = jnp.float32


def _dot(a, b):
    return jnp.dot(a, b, preferred_element_type=F32)


def _rmsnorm(x, g):
    r = lax.rsqrt(jnp.mean(x * x, axis=-1, keepdims=True) + EPS)
    return (x * r) * g


def _params(n_axes):
    return pltpu.CompilerParams(dimension_semantics=("arbitrary",) * n_axes,
                                vmem_limit_bytes=VMEM_LIMIT_BYTES)


def _conv_prompt_kernel(x_ref, g_ref, wh_ref, wb_ref, wc_ref, cw_ref, xn_ref, y_ref, st_ref, carry_ref,
                        *, tm, blocks_per_seq):
    i = pl.program_id(0)
    j = pl.program_id(1)

    @pl.when(j == 0)
    def _():
        xn_ref[...] = _rmsnorm(x_ref[...], g_ref[...]).astype(BF16)

    @pl.when(i % blocks_per_seq == 0)
    def _():
        carry_ref[j] = jnp.zeros(carry_ref.shape[1:], F32)

    xn = xn_ref[...]
    h = _dot(xn, wh_ref[...])
    b = _dot(xn, wb_ref[...])
    c = _dot(xn, wc_ref[...])
    u = c * h
    w0 = cw_ref[0:1, :]
    w1 = cw_ref[1:2, :]
    w2 = cw_ref[2:3, :]
    conv = w0 * pltpu.roll(u, 2, 0) + w1 * pltpu.roll(u, 1, 0) + w2 * u
    y_ref[...] = (b * conv).astype(BF16)
    head = jnp.concatenate([carry_ref[j], u[0:HEAD_ROWS]], axis=0)
    conv_head = w0 * pltpu.roll(head, 2, 0) + w1 * pltpu.roll(head, 1, 0) + w2 * head
    y_ref[0:HEAD_ROWS, :] = (b[0:HEAD_ROWS] * conv_head[SUBLANES:]).astype(BF16)
    tail = u[tm - SUBLANES:tm]
    carry_ref[j] = tail
    st_ref[0] = tail


def _conv_sample_kernel(x_ref, g_ref, wh_ref, wb_ref, wc_ref, cw_ref, s_ref, xn_ref, y_ref, st_ref,
                        *, n_seq, n_t):
    j = pl.program_id(0)

    @pl.when(j == 0)
    def _():
        xn_ref[...] = _rmsnorm(x_ref[...], g_ref[...]).astype(BF16)

    xn = xn_ref[...]
    h = _dot(xn, wh_ref[...])
    b = _dot(xn, wb_ref[...])
    c = _dot(xn, wc_ref[...])
    u = c * h
    w0 = cw_ref[0:1, :]
    w1 = cw_ref[1:2, :]
    w2 = cw_ref[2:3, :]
    n_state = CONV_WIDTH - 1
    ext = [s_ref[k * n_seq:(k + 1) * n_seq, :] for k in range(n_state)]
    ext += [u[t * n_seq:(t + 1) * n_seq] for t in range(n_t)]
    for t in range(n_t):
        conv = w0 * ext[t] + w1 * ext[t + 1] + w2 * ext[t + 2]
        y_ref[t * n_seq:(t + 1) * n_seq, :] = (b[t * n_seq:(t + 1) * n_seq] * conv).astype(BF16)
    for k in range(n_state):
        st_ref[k * n_seq:(k + 1) * n_seq, :] = ext[n_t + k]


def _conv_path_prompt(x2d, g, w_in_bf, conv_w, *, n_batch, seq, d_conv, tm, tn):
    m, d = x2d.shape
    nct = d_conv // tn
    kern = functools.partial(_conv_prompt_kernel, tm=tm, blocks_per_seq=seq // tm)
    return pl.pallas_call(
        kern,
        grid=(m // tm, nct),
        in_specs=[
            pl.BlockSpec((tm, d), lambda i, j: (i, 0)),
            pl.BlockSpec((1, d), lambda i, j: (0, 0)),
            pl.BlockSpec((d, tn), lambda i, j: (0, j)),
            pl.BlockSpec((d, tn), lambda i, j: (0, nct + j)),
            pl.BlockSpec((d, tn), lambda i, j: (0, 2 * nct + j)),
            pl.BlockSpec((CONV_WIDTH, tn), lambda i, j: (0, j)),
        ],
        out_specs=[
            pl.BlockSpec((tm, d), lambda i, j: (i, 0)),
            pl.BlockSpec((tm, tn), lambda i, j: (i, j)),
            pl.BlockSpec((1, SUBLANES, tn), lambda i, j: ((i * tm) // seq, 0, j)),
        ],
        out_shape=[
            jax.ShapeDtypeStruct((m, d), BF16),
            jax.ShapeDtypeStruct((m, d_conv), BF16),
            jax.ShapeDtypeStruct((n_batch, SUBLANES, d_conv), F32),
        ],
        scratch_shapes=[pltpu.VMEM((nct, SUBLANES, tn), F32)],
        compiler_params=_params(2),
        name="conv_path_prompt",
    )(x2d, g, w_in_bf, w_in_bf, w_in_bf, conv_w)


def _conv_path_sample(x2d, g, w_in_bf, conv_w, state_tm, *, n_seq, n_t, d_conv, tn):
    m, d = x2d.shape
    nct = d_conv // tn
    n_state = CONV_WIDTH - 1
    kern = functools.partial(_conv_sample_kernel, n_seq=n_seq, n_t=n_t)
    return pl.pallas_call(
        kern,
        grid=(nct,),
        in_specs=[
            pl.BlockSpec((m, d), lambda j: (0, 0)),
            pl.BlockSpec((1, d), lambda j: (0, 0)),
            pl.BlockSpec((d, tn), lambda j: (0, j)),
            pl.BlockSpec((d, tn), lambda j: (0, nct + j)),
            pl.BlockSpec((d, tn), lambda j: (0, 2 * nct + j)),
            pl.BlockSpec((CONV_WIDTH, tn), lambda j: (0, j)),
            pl.BlockSpec((n_state * n_seq, tn), lambda j: (0, j)),
        ],
        out_specs=[
            pl.BlockSpec((m, d), lambda j: (0, 0)),
            pl.BlockSpec((m, tn), lambda j: (0, j)),
            pl.BlockSpec((n_state * n_seq, tn), lambda j: (0, j)),
        ],
        out_shape=[
            jax.ShapeDtypeStruct((m, d), BF16),
            jax.ShapeDtypeStruct((m, d_conv), BF16),
            jax.ShapeDtypeStruct((n_state * n_seq, d_conv), F32),
        ],
        compiler_params=_params(1),
        name="conv_path_sample",
    )(x2d, g, w_in_bf, w_in_bf, w_in_bf, conv_w, state_tm)


def _pool_prompt_kernel(xn_ref, wv_ref, wp_ref, ps_ref, mixed_ref, st_ref, carry_ref,
                        *, tm, blocks_per_seq, d_group):
    i = pl.program_id(0)
    blk = i % blocks_per_seq

    @pl.when(blk == 0)
    def _():
        carry_ref[...] = jnp.zeros(carry_ref.shape, F32)

    v = _dot(xn_ref[...], wv_ref[...])
    ext = jnp.concatenate([carry_ref[...], v], axis=0)
    tail = v[tm - POOL_CARRY_ROWS:tm]
    carry_ref[...] = tail
    st_ref[0] = tail
    pos = blk * tm + lax.broadcasted_iota(jnp.int32, (tm, 1), 0)
    for g, w in enumerate(POOL_WINDOWS):
        lo, hi = g * d_group, (g + 1) * d_group
        s = ext[:, lo:hi]
        k = 1
        while k < w:
            s = s + pltpu.roll(s, k, 0)
            k *= 2
        cnt = jnp.minimum(pos + 1, w).astype(F32)
        pooled = s[POOL_CARRY_ROWS:] / cnt - v[:, lo:hi]
        mixed = _dot(pooled.astype(BF16), wp_ref[g]) * ps_ref[:, lo:hi]
        mixed_ref[:, lo:hi] = mixed.astype(BF16)


def _pool_sample_kernel(xn_ref, wv_ref, wp_ref, ps_ref, s_ref, mixed_ref, st_ref, *, n_seq, n_t, d_group):
    v = _dot(xn_ref[...], wv_ref[...])
    for g, w in enumerate(POOL_WINDOWS):
        lo, hi = g * d_group, (g + 1) * d_group
        ext = [s_ref[k * n_seq:(k + 1) * n_seq, lo:hi] for k in range(POOL_BUF)]
        ext += [v[t * n_seq:(t + 1) * n_seq, lo:hi] for t in range(n_t)]
        cnt = float(min(PAST_LEN + 1, w))
        for t in range(n_t):
            s = ext[POOL_BUF + t]
            for k in range(1, w):
                s = s + ext[POOL_BUF + t - k]
            pooled = s / cnt - ext[POOL_BUF + t]
            mixed = _dot(pooled.astype(BF16), wp_ref[g]) * ps_ref[:, lo:hi]
            mixed_ref[t * n_seq:(t + 1) * n_seq, lo:hi] = mixed.astype(BF16)
        for k in range(POOL_BUF):
            st_ref[k * n_seq:(k + 1) * n_seq, lo:hi] = ext[n_t + k]


def _pool_path_prompt(xn, w_in_bf, w_pool_bf, pool_scale, *, n_batch, seq, d_pool, v_off, tm):
    m, d = xn.shape
    n_groups = len(POOL_WINDOWS)
    d_group = d_pool // n_groups
    kern = functools.partial(_pool_prompt_kernel, tm=tm, blocks_per_seq=seq // tm, d_group=d_group)
    return pl.pallas_call(
        kern,
        grid=(m // tm,),
        in_specs=[
            pl.BlockSpec((tm, d), lambda i: (i, 0)),
            pl.BlockSpec((d, d_pool), lambda i: (0, v_off // d_pool)),
            pl.BlockSpec((n_groups, d_group, d_group), lambda i: (0, 0, 0)),
            pl.BlockSpec((1, d_pool), lambda i: (0, 0)),
        ],
        out_specs=[
            pl.BlockSpec((tm, d_pool), lambda i: (i, 0)),
            pl.BlockSpec((1, POOL_CARRY_ROWS, d_pool), lambda i: ((i * tm) // seq, 0, 0)),
        ],
        out_shape=[
            jax.ShapeDtypeStruct((m, d_pool), BF16),
            jax.ShapeDtypeStruct((n_batch, POOL_CARRY_ROWS, d_pool), F32),
        ],
        scratch_shapes=[pltpu.VMEM((POOL_CARRY_ROWS, d_pool), F32)],
        compiler_params=_params(1),
        name="pool_path_prompt",
    )(xn, w_in_bf, w_pool_bf, pool_scale)


def _pool_path_sample(xn, w_in_bf, w_pool_bf, pool_scale, state_tm, *, n_seq, n_t, d_pool, v_off):
    m, d = xn.shape
    n_groups = len(POOL_WINDOWS)
    d_group = d_pool // n_groups
    kern = functools.partial(_pool_sample_kernel, n_seq=n_seq, n_t=n_t, d_group=d_group)
    return pl.pallas_call(
        kern,
        grid=(1,),
        in_specs=[
            pl.BlockSpec((m, d), lambda i: (0, 0)),
            pl.BlockSpec((d, d_pool), lambda i: (0, v_off // d_pool)),
            pl.BlockSpec((n_groups, d_group, d_group), lambda i: (0, 0, 0)),
            pl.BlockSpec((1, d_pool), lambda i: (0, 0)),
            pl.BlockSpec((POOL_BUF * n_seq, d_pool), lambda i: (0, 0)),
        ],
        out_specs=[
            pl.BlockSpec((m, d_pool), lambda i: (0, 0)),
            pl.BlockSpec((POOL_BUF * n_seq, d_pool), lambda i: (0, 0)),
        ],
        out_shape=[
            jax.ShapeDtypeStruct((m, d_pool), BF16),
            jax.ShapeDtypeStruct((POOL_BUF * n_seq, d_pool), F32),
        ],
        compiler_params=_params(1),
        name="pool_path_sample",
    )(xn, w_in_bf, w_pool_bf, pool_scale, state_tm)


def _merge_kernel(x_ref, xn_ref, yc_ref, mx_ref, wgc_ref, wgp_ref, wbc_ref, wbp_ref, wo_ref, g_ref,
                  x1_ref, hn_ref):
    j = pl.program_id(1)

    @pl.when(j == 0)
    def _():
        x1_ref[...] = x_ref[...]

    xn = xn_ref[...]
    gc = _dot(xn, wgc_ref[...])
    gp = _dot(xn, wgp_ref[...])
    y_conv = _dot(yc_ref[...], wbc_ref[...])
    y_pool = _dot(mx_ref[...], wbp_ref[...])
    merged = jax.nn.sigmoid(gc) * y_conv + jax.nn.sigmoid(gp) * y_pool
    x1_ref[...] += _dot(merged.astype(BF16), wo_ref[...])

    @pl.when(j == pl.num_programs(1) - 1)
    def _():
        hn_ref[...] = _rmsnorm(x1_ref[...], g_ref[...]).astype(BF16)


def _merge(x2d, xn, ycin, mixed, w_in_bf, w_brc_bf, w_brp_bf, w_o_bf, g_ffn, *, gc_off, gp_off, tm, tn):
    m, d = x2d.shape
    d_conv = ycin.shape[1]
    d_pool = mixed.shape[1]
    return pl.pallas_call(
        _merge_kernel,
        grid=(m // tm, d // tn),
        in_specs=[
            pl.BlockSpec((tm, d), lambda i, j: (i, 0)),
            pl.BlockSpec((tm, d), lambda i, j: (i, 0)),
            pl.BlockSpec((tm, d_conv), lambda i, j: (i, 0)),
            pl.BlockSpec((tm, d_pool), lambda i, j: (i, 0)),
            pl.BlockSpec((d, tn), lambda i, j: (0, gc_off // tn + j)),
            pl.BlockSpec((d, tn), lambda i, j: (0, gp_off // tn + j)),
            pl.BlockSpec((d_conv, tn), lambda i, j: (0, j)),
            pl.BlockSpec((d_pool, tn), lambda i, j: (0, j)),
            pl.BlockSpec((tn, d), lambda i, j: (j, 0)),
            pl.BlockSpec((1, d), lambda i, j: (0, 0)),
        ],
        out_specs=[
            pl.BlockSpec((tm, d), lambda i, j: (i, 0)),
            pl.BlockSpec((tm, d), lambda i, j: (i, 0)),
        ],
        out_shape=[
            jax.ShapeDtypeStruct((m, d), F32),
            jax.ShapeDtypeStruct((m, d), BF16),
        ],
        compiler_params=_params(2),
        name="merge_out_proj",
    )(x2d, xn, ycin, mixed, w_in_bf, w_in_bf, w_brc_bf, w_brp_bf, w_o_bf, g_ffn)


def _ffn_kernel(x1_ref, hn_ref, wg_ref, wu_ref, wd_ref, g_ref, o_ref, acc_ref, *, final_norm):
    f = pl.program_id(1)

    @pl.when(f == 0)
    def _():
        acc_ref[...] = x1_ref[...]

    hn = hn_ref[...]
    gate = _dot(hn, wg_ref[...])
    up = _dot(hn, wu_ref[...])
    ff = (gate * jax.nn.sigmoid(gate)) * up
    acc_ref[...] += _dot(ff.astype(BF16), wd_ref[...])

    @pl.when(f == pl.num_programs(1) - 1)
    def _():
        x2 = acc_ref[...]
        o_ref[...] = _rmsnorm(x2, g_ref[...]) if final_norm else x2


def _ffn(x1, hn, w_gate_bf, w_up_bf, w_down_bf, g_final, *, final_norm, tm, tf):
    m, d = x1.shape
    d_ff = w_gate_bf.shape[1]
    kern = functools.partial(_ffn_kernel, final_norm=final_norm)
    return pl.pallas_call(
        kern,
        grid=(m // tm, d_ff // tf),
        in_specs=[
            pl.BlockSpec((tm, d), lambda i, f: (i, 0)),
            pl.BlockSpec((tm, d), lambda i, f: (i, 0)),
            pl.BlockSpec((d, tf), lambda i, f: (0, f)),
            pl.BlockSpec((d, tf), lambda i, f: (0, f)),
            pl.BlockSpec((tf, d), lambda i, f: (f, 0)),
            pl.BlockSpec((1, d), lambda i, f: (0, 0)),
        ],
        out_specs=pl.BlockSpec((tm, d), lambda i, f: (i, 0)),
        out_shape=jax.ShapeDtypeStruct((m, d), F32),
        scratch_shapes=[pltpu.VMEM((tm, d), F32)],
        compiler_params=_params(2),
        name="swiglu_ffn",
    )(x1, hn, w_gate_bf, w_up_bf, w_down_bf, g_final)


def _tiles(d_conv, d_ff):
    tm = 512
    tn_conv = 512 if d_conv % 512 == 0 else 256
    tn_merge = 256
    tf = 512 if d_ff % 512 == 0 else 256
    return tm, tn_conv, tn_merge, tf


def kernel(x_prompt, x_sample, state_conv, state_pool, norm_mix, w_in, conv_w, w_pool, pool_scale, w_br_conv,
           w_br_pool, w_o, norm_ffn, w_gate, w_up, w_down, norm_final):
    n_batch, seq, d = x_prompt.shape
    n_seq, n_t, _ = x_sample.shape
    depth = w_in.shape[0]
    d_conv = conv_w.shape[2]
    d_pool = pool_scale.shape[1]
    d_ff = w_gate.shape[2]
    v_off = 3 * d_conv
    gc_off = v_off + d_pool
    gp_off = gc_off + d
    tm, tn_conv, tn_merge, tf = _tiles(d_conv, d_ff)
    assert seq % tm == 0 and tm >= POOL_CARRY_ROWS and n_t >= CONV_WIDTH - 1
    assert n_seq % 16 == 0 and v_off % d_pool == 0 and gc_off % tn_merge == 0 and gp_off % tn_merge == 0

    yp = x_prompt.reshape(n_batch * seq, d)
    ys = jnp.transpose(x_sample, (1, 0, 2)).reshape(n_t * n_seq, d)
    g_final = norm_final.reshape(1, d)
    conv_p, pool_p, conv_s, pool_s = [], [], [], []
    for l in range(depth):
        w_in_bf = w_in[l].astype(BF16)
        w_pool_bf = w_pool[l].astype(BF16)
        w_brc_bf = w_br_conv[l].astype(BF16)
        w_brp_bf = w_br_pool[l].astype(BF16)
        w_o_bf = w_o[l].astype(BF16)
        w_gate_bf = w_gate[l].astype(BF16)
        w_up_bf = w_up[l].astype(BF16)
        w_down_bf = w_down[l].astype(BF16)
        g_mix = norm_mix[l].reshape(1, d)
        g_ffn = norm_ffn[l].reshape(1, d)
        ps = pool_scale[l].reshape(1, d_pool)
        last = l == depth - 1

        xn, ycin, st_c = _conv_path_prompt(yp, g_mix, w_in_bf, conv_w[l], n_batch=n_batch, seq=seq,
                                           d_conv=d_conv, tm=tm, tn=tn_conv)
        mixed, st_p = _pool_path_prompt(xn, w_in_bf, w_pool_bf, ps, n_batch=n_batch, seq=seq, d_pool=d_pool,
                                        v_off=v_off, tm=tm)
        x1, hn = _merge(yp, xn, ycin, mixed, w_in_bf, w_brc_bf, w_brp_bf, w_o_bf, g_ffn,
                        gc_off=gc_off, gp_off=gp_off, tm=tm, tn=tn_merge)
        yp = _ffn(x1, hn, w_gate_bf, w_up_bf, w_down_bf, g_final, final_norm=last, tm=tm, tf=tf)
        conv_p.append(st_c[:, SUBLANES - (CONV_WIDTH - 1):, :])
        pool_p.append(st_p[:, POOL_CARRY_ROWS - POOL_BUF:, :])

        sc_tm = jnp.transpose(state_conv[l], (1, 0, 2)).reshape((CONV_WIDTH - 1) * n_seq, d_conv)
        sp_tm = jnp.transpose(state_pool[l], (1, 0, 2)).reshape(POOL_BUF * n_seq, d_pool)
        xn, ycin, st_c = _conv_path_sample(ys, g_mix, w_in_bf, conv_w[l], sc_tm, n_seq=n_seq, n_t=n_t,
                                           d_conv=d_conv, tn=tn_conv)
        mixed, st_p = _pool_path_sample(xn, w_in_bf, w_pool_bf, ps, sp_tm, n_seq=n_seq, n_t=n_t,
                                        d_pool=d_pool, v_off=v_off)
        x1, hn = _merge(ys, xn, ycin, mixed, w_in_bf, w_brc_bf, w_brp_bf, w_o_bf, g_ffn,
                        gc_off=gc_off, gp_off=gp_off, tm=n_t * n_seq, tn=tn_merge)
        ys = _ffn(x1, hn, w_gate_bf, w_up_bf, w_down_bf, g_final, final_norm=last, tm=n_t * n_seq, tf=tf)
        conv_s.append(jnp.transpose(st_c.reshape(CONV_WIDTH - 1, n_seq, d_conv), (1, 0, 2)))
        pool_s.append(jnp.transpose(st_p.reshape(POOL_BUF, n_seq, d_pool), (1, 0, 2)))

    y_prompt = yp.reshape(n_batch, seq, d)
    y_sample = jnp.transpose(ys.reshape(n_t, n_seq, d), (1, 0, 2))
    return (y_prompt, y_sample, jnp.stack(conv_p, axis=0), jnp.stack(pool_p, axis=0),
            jnp.stack(conv_s, axis=0), jnp.stack(pool_s, axis=0))
```

```python
import functools

import jax
import jax.numpy as jnp
from jax import lax
from jax.experimental import pallas as pl
from jax.experimental.pallas import tpu as pltpu

EPS = 1e-6
CONV_WIDTH = 3
POOL_WINDOWS = (2, 4, 8, 16)
POOL_BUF = max(POOL_WINDOWS) - 1
PAST_LEN = 16384

SUBLANES = 8
HEAD_ROWS = 16
POOL_CARRY_ROWS = 16
VMEM_LIMIT_BYTES = 56 * 1024 * 1024

BF16 = jnp.bfloat16
F32 = jnp.float32


def _dot(a, b):
    return jnp.dot(a, b, preferred_element_type=F32)


def _rmsnorm(x, g):
    r = lax.rsqrt(jnp.mean(x * x, axis=-1, keepdims=True) + EPS)
    return (x * r) * g


def _params(n_axes):
    return pltpu.CompilerParams(dimension_semantics=("arbitrary",) * n_axes,
                                vmem_limit_bytes=VMEM_LIMIT_BYTES)


def _conv_prompt_kernel(x_ref, g_ref, wh_ref, wb_ref, wc_ref, cw_ref, xn_ref, y_ref, st_ref, carry_ref,
                        *, tm, blocks_per_seq):
    i = pl.program_id(0)
    j = pl.program_id(1)

    @pl.when(j == 0)
    def _():
        xn_ref[...] = _rmsnorm(x_ref[...], g_ref[...]).astype(BF16)

    @pl.when(i % blocks_per_seq == 0)
    def _():
        carry_ref[j] = jnp.zeros(carry_ref.shape[1:], F32)

    xn = xn_ref[...]
    h = _dot(xn, wh_ref[...])
    b = _dot(xn, wb_ref[...])
    c = _dot(xn, wc_ref[...])
    u = c * h
    w0 = cw_ref[0:1, :]
    w1 = cw_ref[1:2, :]
    w2 = cw_ref[2:3, :]
    conv = w0 * pltpu.roll(u, 2, 0) + w1 * pltpu.roll(u, 1, 0) + w2 * u
    y_ref[...] = (b * conv).astype(BF16)
    head = jnp.concatenate([carry_ref[j], u[0:HEAD_ROWS]], axis=0)
    conv_head = w0 * pltpu.roll(head, 2, 0) + w1 * pltpu.roll(head, 1, 0) + w2 * head
    y_ref[0:HEAD_ROWS, :] = (b[0:HEAD_ROWS] * conv_head[SUBLANES:]).astype(BF16)
    tail = u[tm - SUBLANES:tm]
    carry_ref[j] = tail
    st_ref[0] = tail


def _conv_sample_kernel(x_ref, g_ref, wh_ref, wb_ref, wc_ref, cw_ref, s_ref, xn_ref, y_ref, st_ref,
                        *, n_seq, n_t):
    j = pl.program_id(0)

    @pl.when(j == 0)
    def _():
        xn_ref[...] = _rmsnorm(x_ref[...], g_ref[...]).astype(BF16)

    xn = xn_ref[...]
    h = _dot(xn, wh_ref[...])
    b = _dot(xn, wb_ref[...])
    c = _dot(xn, wc_ref[...])
    u = c * h
    w0 = cw_ref[0:1, :]
    w1 = cw_ref[1:2, :]
    w2 = cw_ref[2:3, :]
    n_state = CONV_WIDTH - 1
    ext = [s_ref[k * n_seq:(k + 1) * n_seq, :] for k in range(n_state)]
    ext += [u[t * n_seq:(t + 1) * n_seq] for t in range(n_t)]
    for t in range(n_t):
        conv = w0 * ext[t] + w1 * ext[t + 1] + w2 * ext[t + 2]
        y_ref[t * n_seq:(t + 1) * n_seq, :] = (b[t * n_seq:(t + 1) * n_seq] * conv).astype(BF16)
    for k in range(n_state):
        st_ref[k * n_seq:(k + 1) * n_seq, :] = ext[n_t + k]


def _conv_path_prompt(x2d, g, w_in_bf, conv_w, *, n_batch, seq, d_conv, tm, tn):
    m, d = x2d.shape
    nct = d_conv // tn
    kern = functools.partial(_conv_prompt_kernel, tm=tm, blocks_per_seq=seq // tm)
    return pl.pallas_call(
        kern,
        grid=(m // tm, nct),
        in_specs=[
            pl.BlockSpec((tm, d), lambda i, j: (i, 0)),
            pl.BlockSpec((1, d), lambda i, j: (0, 0)),
            pl.BlockSpec((d, tn), lambda i, j: (0, j)),
            pl.BlockSpec((d, tn), lambda i, j: (0, nct + j)),
            pl.BlockSpec((d, tn), lambda i, j: (0, 2 * nct + j)),
            pl.BlockSpec((CONV_WIDTH, tn), lambda i, j: (0, j)),
        ],
        out_specs=[
            pl.BlockSpec((tm, d), lambda i, j: (i, 0)),
            pl.BlockSpec((tm, tn), lambda i, j: (i, j)),
            pl.BlockSpec((1, SUBLANES, tn), lambda i, j: (i, 0, j)),
        ],
        out_shape=[
            jax.ShapeDtypeStruct((m, d), BF16),
            jax.ShapeDtypeStruct((m, d_conv), BF16),
            jax.ShapeDtypeStruct((m // tm, SUBLANES, d_conv), F32),
        ],
        scratch_shapes=[pltpu.VMEM((nct, SUBLANES, tn), F32)],
        compiler_params=_params(2),
        name="conv_path_prompt",
    )(x2d, g, w_in_bf, w_in_bf, w_in_bf, conv_w)


def _conv_path_sample(x2d, g, w_in_bf, conv_w, state_tm, *, n_seq, n_t, d_conv, tn):
    m, d = x2d.shape
    nct = d_conv // tn
    n_state = CONV_WIDTH - 1
    kern = functools.partial(_conv_sample_kernel, n_seq=n_seq, n_t=n_t)
    return pl.pallas_call(
        kern,
        grid=(nct,),
        in_specs=[
            pl.BlockSpec((m, d), lambda j: (0, 0)),
            pl.BlockSpec((1, d), lambda j: (0, 0)),
            pl.BlockSpec((d, tn), lambda j: (0, j)),
            pl.BlockSpec((d, tn), lambda j: (0, nct + j)),
            pl.BlockSpec((d, tn), lambda j: (0, 2 * nct + j)),
            pl.BlockSpec((CONV_WIDTH, tn), lambda j: (0, j)),
            pl.BlockSpec((n_state * n_seq, tn), lambda j: (0, j)),
        ],
        out_specs=[
            pl.BlockSpec((m, d), lambda j: (0, 0)),
            pl.BlockSpec((m, tn), lambda j: (0, j)),
            pl.BlockSpec((n_state * n_seq, tn), lambda j: (0, j)),
        ],
        out_shape=[
            jax.ShapeDtypeStruct((m, d), BF16),
            jax.ShapeDtypeStruct((m, d_conv), BF16),
            jax.ShapeDtypeStruct((n_state * n_seq, d_conv), F32),
        ],
        compiler_params=_params(1),
        name="conv_path_sample",
    )(x2d, g, w_in_bf, w_in_bf, w_in_bf, conv_w, state_tm)


def _pool_prompt_kernel(xn_ref, wv_ref, wp_ref, ps_ref, mixed_ref, st_ref, carry_ref,
                        *, tm, blocks_per_seq, d_group):
    i = pl.program_id(0)
    blk = i % blocks_per_seq

    @pl.when(blk == 0)
    def _():
        carry_ref[...] = jnp.zeros(carry_ref.shape, F32)

    v = _dot(xn_ref[...], wv_ref[...])
    ext = jnp.concatenate([carry_ref[...], v], axis=0)
    tail = v[tm - POOL_CARRY_ROWS:tm]
    carry_ref[...] = tail
    st_ref[0] = tail
    pos = blk * tm + lax.broadcasted_iota(jnp.int32, (tm, 1), 0)
    for g, w in enumerate(POOL_WINDOWS):
        lo, hi = g * d_group, (g + 1) * d_group
        s = ext[:, lo:hi]
        k = 1
        while k < w:
            s = s + pltpu.roll(s, k, 0)
            k *= 2
        cnt = jnp.minimum(pos + 1, w).astype(F32)
        pooled = s[POOL_CARRY_ROWS:] / cnt - v[:, lo:hi]
        mixed = _dot(pooled.astype(BF16), wp_ref[g]) * ps_ref[:, lo:hi]
        mixed_ref[:, lo:hi] = mixed.astype(BF16)


def _pool_sample_kernel(xn_ref, wv_ref, wp_ref, ps_ref, s_ref, mixed_ref, st_ref, *, n_seq, n_t, d_group):
    v = _dot(xn_ref[...], wv_ref[...])
    for g, w in enumerate(POOL_WINDOWS):
        lo, hi = g * d_group, (g + 1) * d_group
        ext = [s_ref[k * n_seq:(k + 1) * n_seq, lo:hi] for k in range(POOL_BUF)]
        ext += [v[t * n_seq:(t + 1) * n_seq, lo:hi] for t in range(n_t)]
        cnt = float(min(PAST_LEN + 1, w))
        for t in range(n_t):
            s = ext[POOL_BUF + t]
            for k in range(1, w):
                s = s + ext[POOL_BUF + t - k]
            pooled = s / cnt - ext[POOL_BUF + t]
            mixed = _dot(pooled.astype(BF16), wp_ref[g]) * ps_ref[:, lo:hi]
            mixed_ref[t * n_seq:(t + 1) * n_seq, lo:hi] = mixed.astype(BF16)
        for k in range(POOL_BUF):
            st_ref[k * n_seq:(k + 1) * n_seq, lo:hi] = ext[n_t + k]


def _pool_path_prompt(xn, w_in_bf, w_pool_bf, pool_scale, *, n_batch, seq, d_pool, v_off, tm):
    m, d = xn.shape
    n_groups = len(POOL_WINDOWS)
    d_group = d_pool // n_groups
    kern = functools.partial(_pool_prompt_kernel, tm=tm, blocks_per_seq=seq // tm, d_group=d_group)
    return pl.pallas_call(
        kern,
        grid=(m // tm,),
        in_specs=[
            pl.BlockSpec((tm, d), lambda i: (i, 0)),
            pl.BlockSpec((d, d_pool), lambda i: (0, v_off // d_pool)),
            pl.BlockSpec((n_groups, d_group, d_group), lambda i: (0, 0, 0)),
            pl.BlockSpec((1, d_pool), lambda i: (0, 0)),
        ],
        out_specs=[
            pl.BlockSpec((tm, d_pool), lambda i: (i, 0)),
            pl.BlockSpec((1, POOL_CARRY_ROWS, d_pool), lambda i: ((i * tm) // seq, 0, 0)),
        ],
        out_shape=[
            jax.ShapeDtypeStruct((m, d_pool), BF16),
            jax.ShapeDtypeStruct((n_batch, POOL_CARRY_ROWS, d_pool), F32),
        ],
        scratch_shapes=[pltpu.VMEM((POOL_CARRY_ROWS, d_pool), F32)],
        compiler_params=_params(1),
        name="pool_path_prompt",
    )(xn, w_in_bf, w_pool_bf, pool_scale)


def _pool_path_sample(xn, w_in_bf, w_pool_bf, pool_scale, state_tm, *, n_seq, n_t, d_pool, v_off):
    m, d = xn.shape
    n_groups = len(POOL_WINDOWS)
    d_group = d_pool // n_groups
    kern = functools.partial(_pool_sample_kernel, n_seq=n_seq, n_t=n_t, d_group=d_group)
    return pl.pallas_call(
        kern,
        grid=(1,),
        in_specs=[
            pl.BlockSpec((m, d), lambda i: (0, 0)),
            pl.BlockSpec((d, d_pool), lambda i: (0, v_off // d_pool)),
            pl.BlockSpec((n_groups, d_group, d_group), lambda i: (0, 0, 0)),
            pl.BlockSpec((1, d_pool), lambda i: (0, 0)),
            pl.BlockSpec((POOL_BUF * n_seq, d_pool), lambda i: (0, 0)),
        ],
        out_specs=[
            pl.BlockSpec((m, d_pool), lambda i: (0, 0)),
            pl.BlockSpec((POOL_BUF * n_seq, d_pool), lambda i: (0, 0)),
        ],
        out_shape=[
            jax.ShapeDtypeStruct((m, d_pool), BF16),
            jax.ShapeDtypeStruct((POOL_BUF * n_seq, d_pool), F32),
        ],
        compiler_params=_params(1),
        name="pool_path_sample",
    )(xn, w_in_bf, w_pool_bf, pool_scale, state_tm)


def _merge_kernel(x_ref, xn_ref, yc_ref, mx_ref, wgc_ref, wgp_ref, wbc_ref, wbp_ref, wo_ref, x1_ref):
    j = pl.program_id(1)

    @pl.when(j == 0)
    def _():
        x1_ref[...] = x_ref[...]

    xn = xn_ref[...]
    gc = _dot(xn, wgc_ref[...])
    gp = _dot(xn, wgp_ref[...])
    y_conv = _dot(yc_ref[...], wbc_ref[...])
    y_pool = _dot(mx_ref[...], wbp_ref[...])
    merged = jax.nn.sigmoid(gc) * y_conv + jax.nn.sigmoid(gp) * y_pool
    x1_ref[...] += _dot(merged.astype(BF16), wo_ref[...])


def _merge(x2d, xn, ycin, mixed, w_in_bf, w_brc_bf, w_brp_bf, w_o_bf, *, gc_off, gp_off, tm, tn):
    m, d = x2d.shape
    d_conv = ycin.shape[1]
    d_pool = mixed.shape[1]
    return pl.pallas_call(
        _merge_kernel,
        grid=(m // tm, d // tn),
        in_specs=[
            pl.BlockSpec((tm, d), lambda i, j: (i, 0)),
            pl.BlockSpec((tm, d), lambda i, j: (i, 0)),
            pl.BlockSpec((tm, d_conv), lambda i, j: (i, 0)),
            pl.BlockSpec((tm, d_pool), lambda i, j: (i, 0)),
            pl.BlockSpec((d, tn), lambda i, j: (0, gc_off // tn + j)),
            pl.BlockSpec((d, tn), lambda i, j: (0, gp_off // tn + j)),
            pl.BlockSpec((d_conv, tn), lambda i, j: (0, j)),
            pl.BlockSpec((d_pool, tn), lambda i, j: (0, j)),
            pl.BlockSpec((tn, d), lambda i, j: (j, 0)),
        ],
        out_specs=pl.BlockSpec((tm, d), lambda i, j: (i, 0)),
        out_shape=jax.ShapeDtypeStruct((m, d), F32),
        compiler_params=_params(2),
        name="merge_out_proj",
    )(x2d, xn, ycin, mixed, w_in_bf, w_in_bf, w_brc_bf, w_brp_bf, w_o_bf)


def _ffn_kernel(x1_ref, wg_ref, wu_ref, wd_ref, gf_ref, gl_ref, o_ref, hn_ref, *, final_norm):
    f = pl.program_id(1)

    @pl.when(f == 0)
    def _():
        x1 = x1_ref[...]
        hn_ref[...] = _rmsnorm(x1, gf_ref[...]).astype(BF16)
        o_ref[...] = x1

    hn = hn_ref[...]
    gate = _dot(hn, wg_ref[...])
    up = _dot(hn, wu_ref[...])
    ff = (gate * jax.nn.sigmoid(gate)) * up
    o_ref[...] += _dot(ff.astype(BF16), wd_ref[...])

    if final_norm:
        @pl.when(f == pl.num_programs(1) - 1)
        def _():
            o_ref[...] = _rmsnorm(o_ref[...], gl_ref[...])


def _ffn(x1, w_gate_bf, w_up_bf, w_down_bf, g_ffn, g_final, *, final_norm, tm, tf):
    m, d = x1.shape
    d_ff = w_gate_bf.shape[1]
    kern = functools.partial(_ffn_kernel, final_norm=final_norm)
    return pl.pallas_call(
        kern,
        grid=(m // tm, d_ff // tf),
        in_specs=[
            pl.BlockSpec((tm, d), lambda i, f: (i, 0), pipeline_mode=pl.Buffered(1)),
            pl.BlockSpec((d, tf), lambda i, f: (0, f)),
            pl.BlockSpec((d, tf), lambda i, f: (0, f)),
            pl.BlockSpec((tf, d), lambda i, f: (f, 0)),
            pl.BlockSpec((1, d), lambda i, f: (0, 0)),
            pl.BlockSpec((1, d), lambda i, f: (0, 0)),
        ],
        out_specs=pl.BlockSpec((tm, d), lambda i, f: (i, 0)),
        out_shape=jax.ShapeDtypeStruct((m, d), F32),
        scratch_shapes=[pltpu.VMEM((tm, d), BF16)],
        compiler_params=_params(2),
        name="swiglu_ffn",
    )(x1, w_gate_bf, w_up_bf, w_down_bf, g_ffn, g_final)


def _tiles(d_conv, d_ff):
    tm = 512
    tm_ffn = 1024
    tn_conv = 512 if d_conv % 512 == 0 else 256
    tn_merge = 256
    tf = 512 if d_ff % 512 == 0 else 256
    return tm, tm_ffn, tn_conv, tn_merge, tf


def kernel(x_prompt, x_sample, state_conv, state_pool, norm_mix, w_in, conv_w, w_pool, pool_scale, w_br_conv,
           w_br_pool, w_o, norm_ffn, w_gate, w_up, w_down, norm_final):
    n_batch, seq, d = x_prompt.shape
    n_seq, n_t, _ = x_sample.shape
    depth = w_in.shape[0]
    d_conv = conv_w.shape[2]
    d_pool = pool_scale.shape[1]
    d_ff = w_gate.shape[2]
    v_off = 3 * d_conv
    gc_off = v_off + d_pool
    gp_off = gc_off + d
    tm, tm_ffn, tn_conv, tn_merge, tf = _tiles(d_conv, d_ff)
    assert seq % tm == 0 and (n_batch * seq) % tm_ffn == 0 and tm >= POOL_CARRY_ROWS and n_t >= CONV_WIDTH - 1
    assert n_seq % 16 == 0 and v_off % d_pool == 0 and gc_off % tn_merge == 0 and gp_off % tn_merge == 0

    yp = x_prompt.reshape(n_batch * seq, d)
    ys = jnp.transpose(x_sample, (1, 0, 2)).reshape(n_t * n_seq, d)
    g_final = norm_final.reshape(1, d)
    conv_p, pool_p, conv_s, pool_s = [], [], [], []
    for l in range(depth):
        w_in_bf = w_in[l].astype(BF16)
        w_pool_bf = w_pool[l].astype(BF16)
        w_brc_bf = w_br_conv[l].astype(BF16)
        w_brp_bf = w_br_pool[l].astype(BF16)
        w_o_bf = w_o[l].astype(BF16)
        w_gate_bf = w_gate[l].astype(BF16)
        w_up_bf = w_up[l].astype(BF16)
        w_down_bf = w_down[l].astype(BF16)
        g_mix = norm_mix[l].reshape(1, d)
        g_ffn = norm_ffn[l].reshape(1, d)
        ps = pool_scale[l].reshape(1, d_pool)
        last = l == depth - 1

        xn, ycin, st_c = _conv_path_prompt(yp, g_mix, w_in_bf, conv_w[l], n_batch=n_batch, seq=seq,
                                           d_conv=d_conv, tm=tm, tn=tn_conv)
        mixed, st_p = _pool_path_prompt(xn, w_in_bf, w_pool_bf, ps, n_batch=n_batch, seq=seq, d_pool=d_pool,
                                        v_off=v_off, tm=tm)
        x1 = _merge(yp, xn, ycin, mixed, w_in_bf, w_brc_bf, w_brp_bf, w_o_bf,
                    gc_off=gc_off, gp_off=gp_off, tm=tm, tn=tn_merge)
        yp = _ffn(x1, w_gate_bf, w_up_bf, w_down_bf, g_ffn, g_final, final_norm=last, tm=tm_ffn, tf=tf)
        bps = seq // tm
        conv_p.append(st_c[bps - 1::bps, SUBLANES - (CONV_WIDTH - 1):, :])
        pool_p.append(st_p[:, POOL_CARRY_ROWS - POOL_BUF:, :])

        sc_tm = jnp.transpose(state_conv[l], (1, 0, 2)).reshape((CONV_WIDTH - 1) * n_seq, d_conv)
        sp_tm = jnp.transpose(state_pool[l], (1, 0, 2)).reshape(POOL_BUF * n_seq, d_pool)
        xn, ycin, st_c = _conv_path_sample(ys, g_mix, w_in_bf, conv_w[l], sc_tm, n_seq=n_seq, n_t=n_t,
                                           d_conv=d_conv, tn=tn_conv)
        mixed, st_p = _pool_path_sample(xn, w_in_bf, w_pool_bf, ps, sp_tm, n_seq=n_seq, n_t=n_t,
                                        d_pool=d_pool, v_off=v_off)
        x1 = _merge(ys, xn, ycin, mixed, w_in_bf, w_brc_bf, w_brp_bf, w_o_bf,
                    gc_off=gc_off, gp_off=gp_off, tm=n_t * n_seq, tn=tn_merge)
        ys = _ffn(x1, w_gate_bf, w_up_bf, w_down_bf, g_ffn, g_final, final_norm=last, tm=n_t * n_seq, tf=tf)
        conv_s.append(jnp.transpose(st_c.reshape(CONV_WIDTH - 1, n_seq, d_conv), (1, 0, 2)))
        pool_s.append(jnp.transpose(st_p.reshape(POOL_BUF, n_seq, d_pool), (1, 0, 2)))

    y_prompt = yp.reshape(n_batch, seq, d)
    y_sample = jnp.transpose(ys.reshape(n_t, n_seq, d), (1, 0, 2))
    return (y_prompt, y_sample, jnp.stack(conv_p, axis=0), jnp.stack(pool_p, axis=0),
            jnp.stack(conv_s, axis=0), jnp.stack(pool_s, axis=0))
```

```python
import functools

import jax
import jax.numpy as jnp
from jax import lax
from jax.experimental import pallas as pl
from jax.experimental.pallas import tpu as pltpu

EPS = 1e-6
CONV_WIDTH = 3
POOL_WINDOWS = (2, 4, 8, 16)
POOL_BUF = max(POOL_WINDOWS) - 1
PAST_LEN = 16384

SUBLANES = 8
HEAD_ROWS = 16
POOL_CARRY_ROWS = 16
VMEM_LIMIT_BYTES = 56 * 1024 * 1024

BF16 = jnp.bfloat16
F32 = jnp.float32


def _dot(a, b):
    return jnp.dot(a, b, preferred_element_type=F32)


def _rmsnorm(x, g):
    r = lax.rsqrt(jnp.mean(x * x, axis=-1, keepdims=True) + EPS)
    return (x * r) * g


def _params(n_axes):
    return pltpu.CompilerParams(dimension_semantics=("arbitrary",) * n_axes,
                                vmem_limit_bytes=VMEM_LIMIT_BYTES)


def _conv_prompt_kernel(x_ref, g_ref, wh_ref, wb_ref, wc_ref, cw_ref, xn_ref, y_ref, st_ref, carry_ref,
                        *, tm, blocks_per_seq):
    i = pl.program_id(0)
    j = pl.program_id(1)

    @pl.when(j == 0)
    def _():
        xn_ref[...] = _rmsnorm(x_ref[...], g_ref[...]).astype(BF16)

    @pl.when(i % blocks_per_seq == 0)
    def _():
        carry_ref[j] = jnp.zeros(carry_ref.shape[1:], F32)

    xn = xn_ref[...]
    h = _dot(xn, wh_ref[...])
    b = _dot(xn, wb_ref[...])
    c = _dot(xn, wc_ref[...])
    u = c * h
    w0 = cw_ref[0:1, :]
    w1 = cw_ref[1:2, :]
    w2 = cw_ref[2:3, :]
    conv = w0 * pltpu.roll(u, 2, 0) + w1 * pltpu.roll(u, 1, 0) + w2 * u
    y_ref[...] = (b * conv).astype(BF16)
    head = jnp.concatenate([carry_ref[j], u[0:HEAD_ROWS]], axis=0)
    conv_head = w0 * pltpu.roll(head, 2, 0) + w1 * pltpu.roll(head, 1, 0) + w2 * head
    y_ref[0:HEAD_ROWS, :] = (b[0:HEAD_ROWS] * conv_head[SUBLANES:]).astype(BF16)
    tail = u[tm - SUBLANES:tm]
    carry_ref[j] = tail
    st_ref[0] = tail


def _conv_sample_kernel(x_ref, g_ref, wh_ref, wb_ref, wc_ref, cw_ref, s_ref, xn_ref, y_ref, st_ref,
                        *, n_seq, n_t):
    j = pl.program_id(0)

    @pl.when(j == 0)
    def _():
        xn_ref[...] = _rmsnorm(x_ref[...], g_ref[...]).astype(BF16)

    xn = xn_ref[...]
    h = _dot(xn, wh_ref[...])
    b = _dot(xn, wb_ref[...])
    c = _dot(xn, wc_ref[...])
    u = c * h
    w0 = cw_ref[0:1, :]
    w1 = cw_ref[1:2, :]
    w2 = cw_ref[2:3, :]
    n_state = CONV_WIDTH - 1
    ext = [s_ref[k * n_seq:(k + 1) * n_seq, :] for k in range(n_state)]
    ext += [u[t * n_seq:(t + 1) * n_seq] for t in range(n_t)]
    for t in range(n_t):
        conv = w0 * ext[t] + w1 * ext[t + 1] + w2 * ext[t + 2]
        y_ref[t * n_seq:(t + 1) * n_seq, :] = (b[t * n_seq:(t + 1) * n_seq] * conv).astype(BF16)
    for k in range(n_state):
        st_ref[k * n_seq:(k + 1) * n_seq, :] = ext[n_t + k]


def _conv_path_prompt(x2d, g, w_in_bf, conv_w, *, n_batch, seq, d_conv, tm, tn):
    m, d = x2d.shape
    nct = d_conv // tn
    kern = functools.partial(_conv_prompt_kernel, tm=tm, blocks_per_seq=seq // tm)
    return pl.pallas_call(
        kern,
        grid=(m // tm, nct),
        in_specs=[
            pl.BlockSpec((tm, d), lambda i, j: (i, 0)),
            pl.BlockSpec((1, d), lambda i, j: (0, 0)),
            pl.BlockSpec((d, tn), lambda i, j: (0, j)),
            pl.BlockSpec((d, tn), lambda i, j: (0, nct + j)),
            pl.BlockSpec((d, tn), lambda i, j: (0, 2 * nct + j)),
            pl.BlockSpec((CONV_WIDTH, tn), lambda i, j: (0, j)),
        ],
        out_specs=[
            pl.BlockSpec((tm, d), lambda i, j: (i, 0)),
            pl.BlockSpec((tm, tn), lambda i, j: (i, j)),
            pl.BlockSpec((1, SUBLANES, tn), lambda i, j: (i, 0, j)),
        ],
        out_shape=[
            jax.ShapeDtypeStruct((m, d), BF16),
            jax.ShapeDtypeStruct((m, d_conv), BF16),
            jax.ShapeDtypeStruct((m // tm, SUBLANES, d_conv), F32),
        ],
        scratch_shapes=[pltpu.VMEM((nct, SUBLANES, tn), F32)],
        compiler_params=_params(2),
        name="conv_path_prompt",
    )(x2d, g, w_in_bf, w_in_bf, w_in_bf, conv_w)


def _conv_path_sample(x2d, g, w_in_bf, conv_w, state_tm, *, n_seq, n_t, d_conv, tn):
    m, d = x2d.shape
    nct = d_conv // tn
    n_state = CONV_WIDTH - 1
    kern = functools.partial(_conv_sample_kernel, n_seq=n_seq, n_t=n_t)
    return pl.pallas_call(
        kern,
        grid=(nct,),
        in_specs=[
            pl.BlockSpec((m, d), lambda j: (0, 0)),
            pl.BlockSpec((1, d), lambda j: (0, 0)),
            pl.BlockSpec((d, tn), lambda j: (0, j)),
            pl.BlockSpec((d, tn), lambda j: (0, nct + j)),
            pl.BlockSpec((d, tn), lambda j: (0, 2 * nct + j)),
            pl.BlockSpec((CONV_WIDTH, tn), lambda j: (0, j)),
            pl.BlockSpec((n_state * n_seq, tn), lambda j: (0, j)),
        ],
        out_specs=[
            pl.BlockSpec((m, d), lambda j: (0, 0)),
            pl.BlockSpec((m, tn), lambda j: (0, j)),
            pl.BlockSpec((n_state * n_seq, tn), lambda j: (0, j)),
        ],
        out_shape=[
            jax.ShapeDtypeStruct((m, d), BF16),
            jax.ShapeDtypeStruct((m, d_conv), BF16),
            jax.ShapeDtypeStruct((n_state * n_seq, d_conv), F32),
        ],
        compiler_params=_params(1),
        name="conv_path_sample",
    )(x2d, g, w_in_bf, w_in_bf, w_in_bf, conv_w, state_tm)


def _pool_prompt_kernel(xn_ref, wv_ref, wp_ref, ps_ref, mixed_ref, st_ref, carry_ref,
                        *, tm, blocks_per_seq, d_group):
    i = pl.program_id(0)
    blk = i % blocks_per_seq

    @pl.when(blk == 0)
    def _():
        carry_ref[...] = jnp.zeros(carry_ref.shape, F32)

    v = _dot(xn_ref[...], wv_ref[...])
    ext = jnp.concatenate([carry_ref[...], v], axis=0)
    tail = v[tm - POOL_CARRY_ROWS:tm]
    carry_ref[...] = tail
    st_ref[0] = tail
    pos = blk * tm + lax.broadcasted_iota(jnp.int32, (tm, 1), 0)
    for g, w in enumerate(POOL_WINDOWS):
        lo, hi = g * d_group, (g + 1) * d_group
        s = ext[:, lo:hi]
        k = 1
        while k < w:
            s = s + pltpu.roll(s, k, 0)
            k *= 2
        cnt = jnp.minimum(pos + 1, w).astype(F32)
        pooled = s[POOL_CARRY_ROWS:] / cnt - v[:, lo:hi]
        mixed = _dot(pooled.astype(BF16), wp_ref[g]) * ps_ref[:, lo:hi]
        mixed_ref[:, lo:hi] = mixed.astype(BF16)


def _pool_sample_kernel(xn_ref, wv_ref, wp_ref, ps_ref, s_ref, mixed_ref, st_ref, *, n_seq, n_t, d_group):
    v = _dot(xn_ref[...], wv_ref[...])
    for g, w in enumerate(POOL_WINDOWS):
        lo, hi = g * d_group, (g + 1) * d_group
        ext = [s_ref[k * n_seq:(k + 1) * n_seq, lo:hi] for k in range(POOL_BUF)]
        ext += [v[t * n_seq:(t + 1) * n_seq, lo:hi] for t in range(n_t)]
        cnt = float(min(PAST_LEN + 1, w))
        for t in range(n_t):
            s = ext[POOL_BUF + t]
            for k in range(1, w):
                s = s + ext[POOL_BUF + t - k]
            pooled = s / cnt - ext[POOL_BUF + t]
            mixed = _dot(pooled.astype(BF16), wp_ref[g]) * ps_ref[:, lo:hi]
            mixed_ref[t * n_seq:(t + 1) * n_seq, lo:hi] = mixed.astype(BF16)
        for k in range(POOL_BUF):
            st_ref[k * n_seq:(k + 1) * n_seq, lo:hi] = ext[n_t + k]


def _pool_path_prompt(xn, w_in_bf, w_pool_bf, pool_scale, *, n_batch, seq, d_pool, v_off, tm):
    m, d = xn.shape
    n_groups = len(POOL_WINDOWS)
    d_group = d_pool // n_groups
    kern = functools.partial(_pool_prompt_kernel, tm=tm, blocks_per_seq=seq // tm, d_group=d_group)
    return pl.pallas_call(
        kern,
        grid=(m // tm,),
        in_specs=[
            pl.BlockSpec((tm, d), lambda i: (i, 0)),
            pl.BlockSpec((d, d_pool), lambda i: (0, v_off // d_pool)),
            pl.BlockSpec((n_groups, d_group, d_group), lambda i: (0, 0, 0)),
            pl.BlockSpec((1, d_pool), lambda i: (0, 0)),
        ],
        out_specs=[
            pl.BlockSpec((tm, d_pool), lambda i: (i, 0)),
            pl.BlockSpec((1, POOL_CARRY_ROWS, d_pool), lambda i: ((i * tm) // seq, 0, 0)),
        ],
        out_shape=[
            jax.ShapeDtypeStruct((m, d_pool), BF16),
            jax.ShapeDtypeStruct((n_batch, POOL_CARRY_ROWS, d_pool), F32),
        ],
        scratch_shapes=[pltpu.VMEM((POOL_CARRY_ROWS, d_pool), F32)],
        compiler_params=_params(1),
        name="pool_path_prompt",
    )(xn, w_in_bf, w_pool_bf, pool_scale)


def _pool_path_sample(xn, w_in_bf, w_pool_bf, pool_scale, state_tm, *, n_seq, n_t, d_pool, v_off):
    m, d = xn.shape
    n_groups = len(POOL_WINDOWS)
    d_group = d_pool // n_groups
    kern = functools.partial(_pool_sample_kernel, n_seq=n_seq, n_t=n_t, d_group=d_group)
    return pl.pallas_call(
        kern,
        grid=(1,),
        in_specs=[
            pl.BlockSpec((m, d), lambda i: (0, 0)),
            pl.BlockSpec((d, d_pool), lambda i: (0, v_off // d_pool)),
            pl.BlockSpec((n_groups, d_group, d_group), lambda i: (0, 0, 0)),
            pl.BlockSpec((1, d_pool), lambda i: (0, 0)),
            pl.BlockSpec((POOL_BUF * n_seq, d_pool), lambda i: (0, 0)),
        ],
        out_specs=[
            pl.BlockSpec((m, d_pool), lambda i: (0, 0)),
            pl.BlockSpec((POOL_BUF * n_seq, d_pool), lambda i: (0, 0)),
        ],
        out_shape=[
            jax.ShapeDtypeStruct((m, d_pool), BF16),
            jax.ShapeDtypeStruct((POOL_BUF * n_seq, d_pool), F32),
        ],
        compiler_params=_params(1),
        name="pool_path_sample",
    )(xn, w_in_bf, w_pool_bf, pool_scale, state_tm)


def _gate_merge_kernel(xn_ref, yc_ref, mx_ref, wgc_ref, wgp_ref, wbc_ref, wbp_ref, o_ref):
    xn = xn_ref[...]
    gc = _dot(xn, wgc_ref[...])
    gp = _dot(xn, wgp_ref[...])
    y_conv = _dot(yc_ref[...], wbc_ref[...])
    y_pool = _dot(mx_ref[...], wbp_ref[...])
    o_ref[...] = (jax.nn.sigmoid(gc) * y_conv + jax.nn.sigmoid(gp) * y_pool).astype(BF16)


def _gate_merge(xn, ycin, mixed, w_in_bf, w_brc_bf, w_brp_bf, *, gc_off, gp_off, tm, tn):
    m, d = xn.shape
    d_conv = ycin.shape[1]
    d_pool = mixed.shape[1]
    return pl.pallas_call(
        _gate_merge_kernel,
        grid=(m // tm, d // tn),
        in_specs=[
            pl.BlockSpec((tm, d), lambda i, j: (i, 0)),
            pl.BlockSpec((tm, d_conv), lambda i, j: (i, 0)),
            pl.BlockSpec((tm, d_pool), lambda i, j: (i, 0)),
            pl.BlockSpec((d, tn), lambda i, j: (0, gc_off // tn + j)),
            pl.BlockSpec((d, tn), lambda i, j: (0, gp_off // tn + j)),
            pl.BlockSpec((d_conv, tn), lambda i, j: (0, j)),
            pl.BlockSpec((d_pool, tn), lambda i, j: (0, j)),
        ],
        out_specs=pl.BlockSpec((tm, tn), lambda i, j: (i, j)),
        out_shape=jax.ShapeDtypeStruct((m, d), BF16),
        compiler_params=_params(2),
        name="gate_merge",
    )(xn, ycin, mixed, w_in_bf, w_in_bf, w_brc_bf, w_brp_bf)


def _out_proj_kernel(x_ref, mg_ref, wo_ref, x1_ref):
    x1_ref[...] = x_ref[...] + _dot(mg_ref[...], wo_ref[...])


def _out_proj(x2d, merged, w_o_bf, *, tm, tn):
    m, d = x2d.shape
    return pl.pallas_call(
        _out_proj_kernel,
        grid=(m // tm, d // tn),
        in_specs=[
            pl.BlockSpec((tm, tn), lambda i, j: (i, j)),
            pl.BlockSpec((tm, d), lambda i, j: (i, 0)),
            pl.BlockSpec((d, tn), lambda i, j: (0, j)),
        ],
        out_specs=pl.BlockSpec((tm, tn), lambda i, j: (i, j)),
        out_shape=jax.ShapeDtypeStruct((m, d), F32),
        compiler_params=_params(2),
        name="out_proj_residual",
    )(x2d, merged, w_o_bf)


def _ffn_kernel(x1_ref, wg_ref, wu_ref, wd_ref, gf_ref, gl_ref, o_ref, hn_ref, *, final_norm):
    f = pl.program_id(1)

    @pl.when(f == 0)
    def _():
        x1 = x1_ref[...]
        hn_ref[...] = _rmsnorm(x1, gf_ref[...]).astype(BF16)
        o_ref[...] = x1

    hn = hn_ref[...]
    gate = _dot(hn, wg_ref[...])
    up = _dot(hn, wu_ref[...])
    ff = (gate * jax.nn.sigmoid(gate)) * up
    o_ref[...] += _dot(ff.astype(BF16), wd_ref[...])

    if final_norm:
        @pl.when(f == pl.num_programs(1) - 1)
        def _():
            o_ref[...] = _rmsnorm(o_ref[...], gl_ref[...])


def _ffn(x1, w_gate_bf, w_up_bf, w_down_bf, g_ffn, g_final, *, final_norm, tm, tf):
    m, d = x1.shape
    d_ff = w_gate_bf.shape[1]
    kern = functools.partial(_ffn_kernel, final_norm=final_norm)
    return pl.pallas_call(
        kern,
        grid=(m // tm, d_ff // tf),
        in_specs=[
            pl.BlockSpec((tm, d), lambda i, f: (i, 0)),
            pl.BlockSpec((d, tf), lambda i, f: (0, f)),
            pl.BlockSpec((d, tf), lambda i, f: (0, f)),
            pl.BlockSpec((tf, d), lambda i, f: (f, 0)),
            pl.BlockSpec((1, d), lambda i, f: (0, 0)),
            pl.BlockSpec((1, d), lambda i, f: (0, 0)),
        ],
        out_specs=pl.BlockSpec((tm, d), lambda i, f: (i, 0)),
        out_shape=jax.ShapeDtypeStruct((m, d), F32),
        scratch_shapes=[pltpu.VMEM((tm, d), BF16)],
        compiler_params=_params(2),
        name="swiglu_ffn",
    )(x1, w_gate_bf, w_up_bf, w_down_bf, g_ffn, g_final)


def _tiles(d_conv, d_ff):
    tm = 1024
    tm_pool = 512
    tn_conv = 512 if d_conv % 512 == 0 else 256
    tn_merge = 512
    tn_out = 1024
    tf = 512 if d_ff % 512 == 0 else 256
    return tm, tm_pool, tn_conv, tn_merge, tn_out, tf


def kernel(x_prompt, x_sample, state_conv, state_pool, norm_mix, w_in, conv_w, w_pool, pool_scale, w_br_conv,
           w_br_pool, w_o, norm_ffn, w_gate, w_up, w_down, norm_final):
    n_batch, seq, d = x_prompt.shape
    n_seq, n_t, _ = x_sample.shape
    depth = w_in.shape[0]
    d_conv = conv_w.shape[2]
    d_pool = pool_scale.shape[1]
    d_ff = w_gate.shape[2]
    v_off = 3 * d_conv
    gc_off = v_off + d_pool
    gp_off = gc_off + d
    tm, tm_pool, tn_conv, tn_merge, tn_out, tf = _tiles(d_conv, d_ff)
    assert seq % tm == 0 and seq % tm_pool == 0 and tm_pool >= POOL_CARRY_ROWS and n_t >= CONV_WIDTH - 1
    assert n_seq % 16 == 0 and v_off % d_pool == 0 and gc_off % tn_merge == 0 and gp_off % tn_merge == 0

    yp = x_prompt.reshape(n_batch * seq, d)
    ys = jnp.transpose(x_sample, (1, 0, 2)).reshape(n_t * n_seq, d)
    g_final = norm_final.reshape(1, d)
    conv_p, pool_p, conv_s, pool_s = [], [], [], []
    for l in range(depth):
        w_in_bf = w_in[l].astype(BF16)
        w_pool_bf = w_pool[l].astype(BF16)
        w_brc_bf = w_br_conv[l].astype(BF16)
        w_brp_bf = w_br_pool[l].astype(BF16)
        w_o_bf = w_o[l].astype(BF16)
        w_gate_bf = w_gate[l].astype(BF16)
        w_up_bf = w_up[l].astype(BF16)
        w_down_bf = w_down[l].astype(BF16)
        g_mix = norm_mix[l].reshape(1, d)
        g_ffn = norm_ffn[l].reshape(1, d)
        ps = pool_scale[l].reshape(1, d_pool)
        last = l == depth - 1

        xn, ycin, st_c = _conv_path_prompt(yp, g_mix, w_in_bf, conv_w[l], n_batch=n_batch, seq=seq,
                                           d_conv=d_conv, tm=tm, tn=tn_conv)
        mixed, st_p = _pool_path_prompt(xn, w_in_bf, w_pool_bf, ps, n_batch=n_batch, seq=seq, d_pool=d_pool,
                                        v_off=v_off, tm=tm_pool)
        merged = _gate_merge(xn, ycin, mixed, w_in_bf, w_brc_bf, w_brp_bf,
                             gc_off=gc_off, gp_off=gp_off, tm=tm, tn=tn_merge)
        x1 = _out_proj(yp, merged, w_o_bf, tm=tm, tn=tn_out)
        yp = _ffn(x1, w_gate_bf, w_up_bf, w_down_bf, g_ffn, g_final, final_norm=last, tm=tm, tf=tf)
        bps = seq // tm
        conv_p.append(st_c[bps - 1::bps, SUBLANES - (CONV_WIDTH - 1):, :])
        pool_p.append(st_p[:, POOL_CARRY_ROWS - POOL_BUF:, :])

        sc_tm = jnp.transpose(state_conv[l], (1, 0, 2)).reshape((CONV_WIDTH - 1) * n_seq, d_conv)
        sp_tm = jnp.transpose(state_pool[l], (1, 0, 2)).reshape(POOL_BUF * n_seq, d_pool)
        xn, ycin, st_c = _conv_path_sample(ys, g_mix, w_in_bf, conv_w[l], sc_tm, n_seq=n_seq, n_t=n_t,
                                           d_conv=d_conv, tn=tn_conv)
        mixed, st_p = _pool_path_sample(xn, w_in_bf, w_pool_bf, ps, sp_tm, n_seq=n_seq, n_t=n_t,
                                        d_pool=d_pool, v_off=v_off)
        merged = _gate_merge(xn, ycin, mixed, w_in_bf, w_brc_bf, w_brp_bf,
                             gc_off=gc_off, gp_off=gp_off, tm=n_t * n_seq, tn=tn_merge)
        x1 = _out_proj(ys, merged, w_o_bf, tm=n_t * n_seq, tn=tn_out)
        ys = _ffn(x1, w_gate_bf, w_up_bf, w_down_bf, g_ffn, g_final, final_norm=last, tm=n_t * n_seq, tf=tf)
        conv_s.append(jnp.transpose(st_c.reshape(CONV_WIDTH - 1, n_seq, d_conv), (1, 0, 2)))
        pool_s.append(jnp.transpose(st_p.reshape(POOL_BUF, n_seq, d_pool), (1, 0, 2)))

    y_prompt = yp.reshape(n_batch, seq, d)
    y_sample = jnp.transpose(ys.reshape(n_t, n_seq, d), (1, 0, 2))
    return (y_prompt, y_sample, jnp.stack(conv_p, axis=0), jnp.stack(pool_p, axis=0),
            jnp.stack(conv_s, axis=0), jnp.stack(pool_s, axis=0))
```

```python
import functools

import jax
import jax.numpy as jnp
from jax import lax
from jax.experimental import pallas as pl
from jax.experimental.pallas import tpu as pltpu

EPS = 1e-6
CONV_WIDTH = 3
POOL_WINDOWS = (2, 4, 8, 16)
POOL_BUF = max(POOL_WINDOWS) - 1
PAST_LEN = 16384

SUBLANES = 8
HEAD_ROWS = 16
POOL_CARRY_ROWS = 16
VMEM_LIMIT_BYTES = 56 * 1024 * 1024

BF16 = jnp.bfloat16
F32 = jnp.float32


def _dot(a, b):
    return jnp.dot(a, b, preferred_element_type=F32)


def _rmsnorm(x, g):
    r = lax.rsqrt(jnp.mean(x * x, axis=-1, keepdims=True) + EPS)
    return (x * r) * g


def _params(n_axes):
    return pltpu.CompilerParams(dimension_semantics=("arbitrary",) * n_axes,
                                vmem_limit_bytes=VMEM_LIMIT_BYTES)


def _cast_riders(src_refs, dst_refs):
    for src, dst in zip(src_refs, dst_refs):
        dst[...] = src[...].astype(BF16)


def _slab_rider(w, n_steps, step_of):
    rows, cols = w.shape
    slab = rows // n_steps
    assert slab * n_steps == rows and slab % HEAD_ROWS == 0
    spec = pl.BlockSpec((slab, cols), lambda *ids: (step_of(*ids), 0))
    return spec, spec, jax.ShapeDtypeStruct((rows, cols), BF16)


def _tail_rider(w, col0, n_steps, step_of, *, width):
    rows, cols = w.shape
    n_col = (cols - col0) // width
    assert n_col * width == cols - col0 and col0 % width == 0 and n_col <= n_steps
    n_row = 1
    while n_row * 2 * n_col <= n_steps and rows % (n_row * 2) == 0:
        n_row *= 2
    rb = rows // n_row
    assert rb % HEAD_ROWS == 0
    n_blk = n_row * n_col

    def blk(*ids):
        return jnp.minimum(step_of(*ids), n_blk - 1)

    in_spec = pl.BlockSpec((rb, width), lambda *ids: (blk(*ids) // n_col, col0 // width + blk(*ids) % n_col))
    out_spec = pl.BlockSpec((rb, width), lambda *ids: (blk(*ids) // n_col, blk(*ids) % n_col))
    return in_spec, out_spec, jax.ShapeDtypeStruct((rows, cols - col0), BF16)


def _conv_prompt_kernel(x_ref, g_ref, wh_ref, wb_ref, wc_ref, cw_ref, *rest, tm, blocks_per_seq, n_riders):
    rider_src = rest[:n_riders]
    xn_ref, y_ref, st_ref = rest[n_riders:n_riders + 3]
    rider_dst = rest[n_riders + 3:2 * n_riders + 3]
    carry_ref = rest[2 * n_riders + 3]
    i = pl.program_id(0)
    j = pl.program_id(1)
    _cast_riders(rider_src, rider_dst)

    @pl.when(j == 0)
    def _():
        xn_ref[...] = _rmsnorm(x_ref[...], g_ref[...]).astype(BF16)

    @pl.when(i % blocks_per_seq == 0)
    def _():
        carry_ref[j] = jnp.zeros(carry_ref.shape[1:], F32)

    xn = xn_ref[...]
    h = _dot(xn, wh_ref[...])
    b = _dot(xn, wb_ref[...])
    c = _dot(xn, wc_ref[...])
    u = c * h
    w0 = cw_ref[0:1, :]
    w1 = cw_ref[1:2, :]
    w2 = cw_ref[2:3, :]
    conv = w0 * pltpu.roll(u, 2, 0) + w1 * pltpu.roll(u, 1, 0) + w2 * u
    y_ref[...] = (b * conv).astype(BF16)
    head = jnp.concatenate([carry_ref[j], u[0:HEAD_ROWS]], axis=0)
    conv_head = w0 * pltpu.roll(head, 2, 0) + w1 * pltpu.roll(head, 1, 0) + w2 * head
    y_ref[0:HEAD_ROWS, :] = (b[0:HEAD_ROWS] * conv_head[SUBLANES:]).astype(BF16)
    tail = u[tm - SUBLANES:tm]
    carry_ref[j] = tail
    st_ref[0] = tail


def _conv_sample_kernel(x_ref, g_ref, wh_ref, wb_ref, wc_ref, cw_ref, s_ref, xn_ref, y_ref, st_ref,
                        *, n_seq, n_t):
    j = pl.program_id(0)

    @pl.when(j == 0)
    def _():
        xn_ref[...] = _rmsnorm(x_ref[...], g_ref[...]).astype(BF16)

    xn = xn_ref[...]
    h = _dot(xn, wh_ref[...])
    b = _dot(xn, wb_ref[...])
    c = _dot(xn, wc_ref[...])
    u = c * h
    w0 = cw_ref[0:1, :]
    w1 = cw_ref[1:2, :]
    w2 = cw_ref[2:3, :]
    n_state = CONV_WIDTH - 1
    ext = [s_ref[k * n_seq:(k + 1) * n_seq, :] for k in range(n_state)]
    ext += [u[t * n_seq:(t + 1) * n_seq] for t in range(n_t)]
    for t in range(n_t):
        conv = w0 * ext[t] + w1 * ext[t + 1] + w2 * ext[t + 2]
        y_ref[t * n_seq:(t + 1) * n_seq, :] = (b[t * n_seq:(t + 1) * n_seq] * conv).astype(BF16)
    for k in range(n_state):
        st_ref[k * n_seq:(k + 1) * n_seq, :] = ext[n_t + k]


def _conv_path_prompt(x2d, g, w_in_bf, conv_w, w_in, slab_weights, *, seq, d_conv, tm, tn):
    m, d = x2d.shape
    nct = d_conv // tn
    n_steps = (m // tm) * nct
    step_of = lambda i, j: i * nct + j
    riders = [_tail_rider(w_in, 3 * d_conv, n_steps, step_of, width=tn * 2)]
    riders += [_slab_rider(w, n_steps, step_of) for w in slab_weights]
    kern = functools.partial(_conv_prompt_kernel, tm=tm, blocks_per_seq=seq // tm, n_riders=len(riders))
    return pl.pallas_call(
        kern,
        grid=(m // tm, nct),
        in_specs=[
            pl.BlockSpec((tm, d), lambda i, j: (i, 0)),
            pl.BlockSpec((1, d), lambda i, j: (0, 0)),
            pl.BlockSpec((d, tn), lambda i, j: (0, j)),
            pl.BlockSpec((d, tn), lambda i, j: (0, nct + j)),
            pl.BlockSpec((d, tn), lambda i, j: (0, 2 * nct + j)),
            pl.BlockSpec((CONV_WIDTH, tn), lambda i, j: (0, j)),
        ] + [r[0] for r in riders],
        out_specs=[
            pl.BlockSpec((tm, d), lambda i, j: (i, 0)),
            pl.BlockSpec((tm, tn), lambda i, j: (i, j)),
            pl.BlockSpec((1, SUBLANES, tn), lambda i, j: (i, 0, j)),
        ] + [r[1] for r in riders],
        out_shape=[
            jax.ShapeDtypeStruct((m, d), BF16),
            jax.ShapeDtypeStruct((m, d_conv), BF16),
            jax.ShapeDtypeStruct((m // tm, SUBLANES, d_conv), F32),
        ] + [r[2] for r in riders],
        scratch_shapes=[pltpu.VMEM((nct, SUBLANES, tn), F32)],
        compiler_params=_params(2),
        name="conv_path_prompt",
    )(x2d, g, w_in_bf, w_in_bf, w_in_bf, conv_w, w_in, *slab_weights)


def _conv_path_sample(x2d, g, w_in_bf, conv_w, state_tm, *, n_seq, n_t, d_conv, tn):
    m, d = x2d.shape
    nct = d_conv // tn
    n_state = CONV_WIDTH - 1
    kern = functools.partial(_conv_sample_kernel, n_seq=n_seq, n_t=n_t)
    return pl.pallas_call(
        kern,
        grid=(nct,),
        in_specs=[
            pl.BlockSpec((m, d), lambda j: (0, 0)),
            pl.BlockSpec((1, d), lambda j: (0, 0)),
            pl.BlockSpec((d, tn), lambda j: (0, j)),
            pl.BlockSpec((d, tn), lambda j: (0, nct + j)),
            pl.BlockSpec((d, tn), lambda j: (0, 2 * nct + j)),
            pl.BlockSpec((CONV_WIDTH, tn), lambda j: (0, j)),
            pl.BlockSpec((n_state * n_seq, tn), lambda j: (0, j)),
        ],
        out_specs=[
            pl.BlockSpec((m, d), lambda j: (0, 0)),
            pl.BlockSpec((m, tn), lambda j: (0, j)),
            pl.BlockSpec((n_state * n_seq, tn), lambda j: (0, j)),
        ],
        out_shape=[
            jax.ShapeDtypeStruct((m, d), BF16),
            jax.ShapeDtypeStruct((m, d_conv), BF16),
            jax.ShapeDtypeStruct((n_state * n_seq, d_conv), F32),
        ],
        compiler_params=_params(1),
        name="conv_path_sample",
    )(x2d, g, w_in_bf, w_in_bf, w_in_bf, conv_w, state_tm)


def _pool_prompt_kernel(xn_ref, wv_ref, wp_ref, ps_ref, *rest, tm, blocks_per_seq, d_group, n_riders):
    rider_src = rest[:n_riders]
    mixed_ref, st_ref = rest[n_riders:n_riders + 2]
    rider_dst = rest[n_riders + 2:2 * n_riders + 2]
    carry_ref = rest[2 * n_riders + 2]
    i = pl.program_id(0)
    blk = i % blocks_per_seq
    _cast_riders(rider_src, rider_dst)

    @pl.when(blk == 0)
    def _():
        carry_ref[...] = jnp.zeros(carry_ref.shape, F32)

    v = _dot(xn_ref[...], wv_ref[...])
    ext = jnp.concatenate([carry_ref[...], v], axis=0)
    tail = v[tm - POOL_CARRY_ROWS:tm]
    carry_ref[...] = tail
    st_ref[0] = tail
    pos = blk * tm + lax.broadcasted_iota(jnp.int32, (tm, 1), 0)
    for g, w in enumerate(POOL_WINDOWS):
        lo, hi = g * d_group, (g + 1) * d_group
        s = ext[:, lo:hi]
        k = 1
        while k < w:
            s = s + pltpu.roll(s, k, 0)
            k *= 2
        cnt = jnp.minimum(pos + 1, w).astype(F32)
        pooled = s[POOL_CARRY_ROWS:] / cnt - v[:, lo:hi]
        mixed = _dot(pooled.astype(BF16), wp_ref[g].astype(BF16)) * ps_ref[:, lo:hi]
        mixed_ref[:, lo:hi] = mixed.astype(BF16)


def _pool_sample_kernel(xn_ref, wv_ref, wp_ref, ps_ref, s_ref, mixed_ref, st_ref, *, n_seq, n_t, d_group):
    v = _dot(xn_ref[...], wv_ref[...])
    for g, w in enumerate(POOL_WINDOWS):
        lo, hi = g * d_group, (g + 1) * d_group
        ext = [s_ref[k * n_seq:(k + 1) * n_seq, lo:hi] for k in range(POOL_BUF)]
        ext += [v[t * n_seq:(t + 1) * n_seq, lo:hi] for t in range(n_t)]
        cnt = float(min(PAST_LEN + 1, w))
        wp = wp_ref[g].astype(BF16)
        for t in range(n_t):
            s = ext[POOL_BUF + t]
            for k in range(1, w):
                s = s + ext[POOL_BUF + t - k]
            pooled = s / cnt - ext[POOL_BUF + t]
            mixed = _dot(pooled.astype(BF16), wp) * ps_ref[:, lo:hi]
            mixed_ref[t * n_seq:(t + 1) * n_seq, lo:hi] = mixed.astype(BF16)
        for k in range(POOL_BUF):
            st_ref[k * n_seq:(k + 1) * n_seq, lo:hi] = ext[n_t + k]


def _pool_path_prompt(xn, w_tail_bf, w_pool, pool_scale, slab_weights, *, n_batch, seq, d_pool, tm):
    m, d = xn.shape
    n_groups = len(POOL_WINDOWS)
    d_group = d_pool // n_groups
    riders = [_slab_rider(w, m // tm, lambda i: i) for w in slab_weights]
    kern = functools.partial(_pool_prompt_kernel, tm=tm, blocks_per_seq=seq // tm, d_group=d_group,
                             n_riders=len(riders))
    return pl.pallas_call(
        kern,
        grid=(m // tm,),
        in_specs=[
            pl.BlockSpec((tm, d), lambda i: (i, 0)),
            pl.BlockSpec((d, d_pool), lambda i: (0, 0)),
            pl.BlockSpec((n_groups, d_group, d_group), lambda i: (0, 0, 0)),
            pl.BlockSpec((1, d_pool), lambda i: (0, 0)),
        ] + [r[0] for r in riders],
        out_specs=[
            pl.BlockSpec((tm, d_pool), lambda i: (i, 0)),
            pl.BlockSpec((1, POOL_CARRY_ROWS, d_pool), lambda i: ((i * tm) // seq, 0, 0)),
        ] + [r[1] for r in riders],
        out_shape=[
            jax.ShapeDtypeStruct((m, d_pool), BF16),
            jax.ShapeDtypeStruct((n_batch, POOL_CARRY_ROWS, d_pool), F32),
        ] + [r[2] for r in riders],
        scratch_shapes=[pltpu.VMEM((POOL_CARRY_ROWS, d_pool), F32)],
        compiler_params=_params(1),
        name="pool_path_prompt",
    )(xn, w_tail_bf, w_pool, pool_scale, *slab_weights)


def _pool_path_sample(xn, w_tail_bf, w_pool, pool_scale, state_tm, *, n_seq, n_t, d_pool):
    m, d = xn.shape
    n_groups = len(POOL_WINDOWS)
    d_group = d_pool // n_groups
    kern = functools.partial(_pool_sample_kernel, n_seq=n_seq, n_t=n_t, d_group=d_group)
    return pl.pallas_call(
        kern,
        grid=(1,),
        in_specs=[
            pl.BlockSpec((m, d), lambda i: (0, 0)),
            pl.BlockSpec((d, d_pool), lambda i: (0, 0)),
            pl.BlockSpec((n_groups, d_group, d_group), lambda i: (0, 0, 0)),
            pl.BlockSpec((1, d_pool), lambda i: (0, 0)),
            pl.BlockSpec((POOL_BUF * n_seq, d_pool), lambda i: (0, 0)),
        ],
        out_specs=[
            pl.BlockSpec((m, d_pool), lambda i: (0, 0)),
            pl.BlockSpec((POOL_BUF * n_seq, d_pool), lambda i: (0, 0)),
        ],
        out_shape=[
            jax.ShapeDtypeStruct((m, d_pool), BF16),
            jax.ShapeDtypeStruct((POOL_BUF * n_seq, d_pool), F32),
        ],
        compiler_params=_params(1),
        name="pool_path_sample",
    )(xn, w_tail_bf, w_pool, pool_scale, state_tm)


def _gate_merge_kernel(xn_ref, yc_ref, mx_ref, wgc_ref, wgp_ref, wbc_ref, wbp_ref, *rest, n_riders):
    rider_src = rest[:n_riders]
    o_ref = rest[n_riders]
    rider_dst = rest[n_riders + 1:2 * n_riders + 1]
    _cast_riders(rider_src, rider_dst)
    xn = xn_ref[...]
    gc = _dot(xn, wgc_ref[...])
    gp = _dot(xn, wgp_ref[...])
    y_conv = _dot(yc_ref[...], wbc_ref[...])
    y_pool = _dot(mx_ref[...], wbp_ref[...])
    o_ref[...] = (jax.nn.sigmoid(gc) * y_conv + jax.nn.sigmoid(gp) * y_pool).astype(BF16)


def _gate_merge(xn, ycin, mixed, w_tail_bf, w_brc_bf, w_brp_bf, slab_weights, *, gc_off, gp_off, tm, tn):
    m, d = xn.shape
    d_conv = ycin.shape[1]
    d_pool = mixed.shape[1]
    nct = d // tn
    riders = [_slab_rider(w, (m // tm) * nct, lambda i, j: i * nct + j) for w in slab_weights]
    out = pl.pallas_call(
        functools.partial(_gate_merge_kernel, n_riders=len(riders)),
        grid=(m // tm, nct),
        in_specs=[
            pl.BlockSpec((tm, d), lambda i, j: (i, 0)),
            pl.BlockSpec((tm, d_conv), lambda i, j: (i, 0)),
            pl.BlockSpec((tm, d_pool), lambda i, j: (i, 0)),
            pl.BlockSpec((d, tn), lambda i, j: (0, gc_off // tn + j)),
            pl.BlockSpec((d, tn), lambda i, j: (0, gp_off // tn + j)),
            pl.BlockSpec((d_conv, tn), lambda i, j: (0, j)),
            pl.BlockSpec((d_pool, tn), lambda i, j: (0, j)),
        ] + [r[0] for r in riders],
        out_specs=[pl.BlockSpec((tm, tn), lambda i, j: (i, j))] + [r[1] for r in riders],
        out_shape=[jax.ShapeDtypeStruct((m, d), BF16)] + [r[2] for r in riders],
        compiler_params=_params(2),
        name="gate_merge",
    )(xn, ycin, mixed, w_tail_bf, w_tail_bf, w_brc_bf, w_brp_bf, *slab_weights)
    return out[0], out[1:]


def _out_proj_kernel(x_ref, mg_ref, wo_ref, x1_ref):
    x1_ref[...] = x_ref[...] + _dot(mg_ref[...], wo_ref[...])


def _out_proj(x2d, merged, w_o_bf, *, tm, tn):
    m, d = x2d.shape
    return pl.pallas_call(
        _out_proj_kernel,
        grid=(m // tm, d // tn),
        in_specs=[
            pl.BlockSpec((tm, tn), lambda i, j: (i, j)),
            pl.BlockSpec((tm, d), lambda i, j: (i, 0)),
            pl.BlockSpec((d, tn), lambda i, j: (0, j)),
        ],
        out_specs=pl.BlockSpec((tm, tn), lambda i, j: (i, j)),
        out_shape=jax.ShapeDtypeStruct((m, d), F32),
        compiler_params=_params(2),
        name="out_proj_residual",
    )(x2d, merged, w_o_bf)


def _ffn_kernel(x1_ref, wg_ref, wu_ref, wd_ref, gf_ref, gl_ref, o_ref, hn_ref, *, final_norm):
    f = pl.program_id(1)

    @pl.when(f == 0)
    def _():
        x1 = x1_ref[...]
        hn_ref[...] = _rmsnorm(x1, gf_ref[...]).astype(BF16)
        o_ref[...] = x1

    hn = hn_ref[...]
    gate = _dot(hn, wg_ref[...])
    up = _dot(hn, wu_ref[...])
    ff = (gate * jax.nn.sigmoid(gate)) * up
    o_ref[...] += _dot(ff.astype(BF16), wd_ref[...])

    if final_norm:
        @pl.when(f == pl.num_programs(1) - 1)
        def _():
            o_ref[...] = _rmsnorm(o_ref[...], gl_ref[...])


def _ffn(x1, w_gate_bf, w_up_bf, w_down_bf, g_ffn, g_final, *, final_norm, tm, tf):
    m, d = x1.shape
    d_ff = w_gate_bf.shape[1]
    kern = functools.partial(_ffn_kernel, final_norm=final_norm)
    return pl.pallas_call(
        kern,
        grid=(m // tm, d_ff // tf),
        in_specs=[
            pl.BlockSpec((tm, d), lambda i, f: (i, 0)),
            pl.BlockSpec((d, tf), lambda i, f: (0, f)),
            pl.BlockSpec((d, tf), lambda i, f: (0, f)),
            pl.BlockSpec((tf, d), lambda i, f: (f, 0)),
            pl.BlockSpec((1, d), lambda i, f: (0, 0)),
            pl.BlockSpec((1, d), lambda i, f: (0, 0)),
        ],
        out_specs=pl.BlockSpec((tm, d), lambda i, f: (i, 0)),
        out_shape=jax.ShapeDtypeStruct((m, d), F32),
        scratch_shapes=[pltpu.VMEM((tm, d), BF16)],
        compiler_params=_params(2),
        name="swiglu_ffn",
    )(x1, w_gate_bf, w_up_bf, w_down_bf, g_ffn, g_final)


def _tiles(d_conv, d_ff):
    tm = 1024
    tm_pool = 512
    tn_conv = 512 if d_conv % 512 == 0 else 256
    tn_merge = 512
    tn_out = 1024
    tf = 512 if d_ff % 512 == 0 else 256
    return tm, tm_pool, tn_conv, tn_merge, tn_out, tf


def kernel(x_prompt, x_sample, state_conv, state_pool, norm_mix, w_in, conv_w, w_pool, pool_scale, w_br_conv,
           w_br_pool, w_o, norm_ffn, w_gate, w_up, w_down, norm_final):
    n_batch, seq, d = x_prompt.shape
    n_seq, n_t, _ = x_sample.shape
    depth = w_in.shape[0]
    d_conv = conv_w.shape[2]
    d_pool = pool_scale.shape[1]
    d_ff = w_gate.shape[2]
    v_off = 3 * d_conv
    gc_off = d_pool
    gp_off = gc_off + d
    tm, tm_pool, tn_conv, tn_merge, tn_out, tf = _tiles(d_conv, d_ff)
    assert seq % tm == 0 and seq % tm_pool == 0 and tm_pool >= POOL_CARRY_ROWS and n_t >= CONV_WIDTH - 1
    assert n_seq % 16 == 0 and gc_off % tn_merge == 0 and gp_off % tn_merge == 0

    yp = x_prompt.reshape(n_batch * seq, d)
    ys = jnp.transpose(x_sample, (1, 0, 2)).reshape(n_t * n_seq, d)
    g_final = norm_final.reshape(1, d)
    conv_p, pool_p, conv_s, pool_s = [], [], [], []
    for l in range(depth):
        w_hbc_bf = w_in[l][:, :v_off].astype(BF16)
        g_mix = norm_mix[l].reshape(1, d)
        g_ffn = norm_ffn[l].reshape(1, d)
        ps = pool_scale[l].reshape(1, d_pool)
        last = l == depth - 1

        xn, ycin, st_c, w_tail_bf, w_brc_bf, w_brp_bf, w_o_bf = _conv_path_prompt(
            yp, g_mix, w_hbc_bf, conv_w[l], w_in[l], (w_br_conv[l], w_br_pool[l], w_o[l]),
            seq=seq, d_conv=d_conv, tm=tm, tn=tn_conv)
        mixed, st_p, w_down_bf = _pool_path_prompt(xn, w_tail_bf, w_pool[l], ps, (w_down[l],), n_batch=n_batch,
                                                   seq=seq, d_pool=d_pool, tm=tm_pool)
        merged, (w_gate_bf, w_up_bf) = _gate_merge(xn, ycin, mixed, w_tail_bf, w_brc_bf, w_brp_bf,
                                                   (w_gate[l], w_up[l]), gc_off=gc_off, gp_off=gp_off,
                                                   tm=tm, tn=tn_merge)
        x1 = _out_proj(yp, merged, w_o_bf, tm=tm, tn=tn_out)
        yp = _ffn(x1, w_gate_bf, w_up_bf, w_down_bf, g_ffn, g_final, final_norm=last, tm=tm, tf=tf)
        bps = seq // tm
        conv_p.append(st_c[bps - 1::bps, SUBLANES - (CONV_WIDTH - 1):, :])
        pool_p.append(st_p[:, POOL_CARRY_ROWS - POOL_BUF:, :])

        sc_tm = jnp.transpose(state_conv[l], (1, 0, 2)).reshape((CONV_WIDTH - 1) * n_seq, d_conv)
        sp_tm = jnp.transpose(state_pool[l], (1, 0, 2)).reshape(POOL_BUF * n_seq, d_pool)
        xn, ycin, st_c = _conv_path_sample(ys, g_mix, w_hbc_bf, conv_w[l], sc_tm, n_seq=n_seq, n_t=n_t,
                                           d_conv=d_conv, tn=tn_conv)
        mixed, st_p = _pool_path_sample(xn, w_tail_bf, w_pool[l], ps, sp_tm, n_seq=n_seq, n_t=n_t, d_pool=d_pool)
        merged, _ = _gate_merge(xn, ycin, mixed, w_tail_bf, w_brc_bf, w_brp_bf, (),
                                gc_off=gc_off, gp_off=gp_off, tm=n_t * n_seq, tn=tn_merge)
        x1 = _out_proj(ys, merged, w_o_bf, tm=n_t * n_seq, tn=tn_out)
        ys = _ffn(x1, w_gate_bf, w_up_bf, w_down_bf, g_ffn, g_final, final_norm=last, tm=n_t * n_seq, tf=tf)
        conv_s.append(jnp.transpose(st_c.reshape(CONV_WIDTH - 1, n_seq, d_conv), (1, 0, 2)))
        pool_s.append(jnp.transpose(st_p.reshape(POOL_BUF, n_seq, d_pool), (1, 0, 2)))

    y_prompt = yp.reshape(n_batch, seq, d)
    y_sample = jnp.transpose(ys.reshape(n_t, n_seq, d), (1, 0, 2))
    return (y_prompt, y_sample, jnp.stack(conv_p, axis=0), jnp.stack(pool_p, axis=0),
            jnp.stack(conv_s, axis=0), jnp.stack(pool_s, axis=0))
```

```python
import functools

import jax
import jax.numpy as jnp
from jax import lax
from jax.experimental import pallas as pl
from jax.experimental.pallas import tpu as pltpu

EPS = 1e-6
CONV_WIDTH = 3
POOL_WINDOWS = (2, 4, 8, 16)
POOL_BUF = max(POOL_WINDOWS) - 1
PAST_LEN = 16384

SUBLANES = 8
HEAD_ROWS = 16
POOL_CARRY_ROWS = 16
VMEM_LIMIT_BYTES = 56 * 1024 * 1024

BF16 = jnp.bfloat16
F32 = jnp.float32


def _dot(a, b):
    return jnp.dot(a, b, preferred_element_type=F32)


def _rmsnorm(x, g):
    r = lax.rsqrt(jnp.mean(x * x, axis=-1, keepdims=True) + EPS)
    return (x * r) * g


def _params(n_axes):
    return pltpu.CompilerParams(dimension_semantics=("arbitrary",) * n_axes,
                                vmem_limit_bytes=VMEM_LIMIT_BYTES)


def _cast_riders(src_refs, dst_refs):
    for src, dst in zip(src_refs, dst_refs):
        dst[...] = src[...].astype(BF16)


def _slab_rider(w, n_steps, step_of):
    rows, cols = w.shape
    slab = rows // n_steps
    assert slab * n_steps == rows and slab % HEAD_ROWS == 0
    spec = pl.BlockSpec((slab, cols), lambda *ids: (step_of(*ids), 0))
    return spec, spec, jax.ShapeDtypeStruct((rows, cols), BF16)


def _tail_rider(w, col0, n_steps, step_of, *, width):
    rows, cols = w.shape
    n_col = (cols - col0) // width
    assert n_col * width == cols - col0 and col0 % width == 0 and n_col <= n_steps
    n_row = 1
    while n_row * 2 * n_col <= n_steps and rows % (n_row * 2) == 0:
        n_row *= 2
    rb = rows // n_row
    assert rb % HEAD_ROWS == 0
    n_blk = n_row * n_col

    def blk(*ids):
        return jnp.minimum(step_of(*ids), n_blk - 1)

    in_spec = pl.BlockSpec((rb, width), lambda *ids: (blk(*ids) // n_col, col0 // width + blk(*ids) % n_col))
    out_spec = pl.BlockSpec((rb, width), lambda *ids: (blk(*ids) // n_col, blk(*ids) % n_col))
    return in_spec, out_spec, jax.ShapeDtypeStruct((rows, cols - col0), BF16)


def _conv_prompt_kernel(x_ref, g_ref, wh_ref, wb_ref, wc_ref, cw_ref, *rest, tm, blocks_per_seq, n_riders):
    rider_src = rest[:n_riders]
    xn_ref, y_ref, st_ref = rest[n_riders:n_riders + 3]
    rider_dst = rest[n_riders + 3:2 * n_riders + 3]
    carry_ref = rest[2 * n_riders + 3]
    i = pl.program_id(0)
    j = pl.program_id(1)

    @pl.when(j == 0)
    def _():
        xn_ref[...] = _rmsnorm(x_ref[...], g_ref[...]).astype(BF16)

    @pl.when(i % blocks_per_seq == 0)
    def _():
        carry_ref[j] = jnp.zeros(carry_ref.shape[1:], F32)

    xn = xn_ref[...]
    c = _dot(xn, wc_ref[...])
    h = _dot(xn, wh_ref[...])
    u = c * h
    b = _dot(xn, wb_ref[...])
    w0 = cw_ref[0:1, :]
    w1 = cw_ref[1:2, :]
    w2 = cw_ref[2:3, :]
    conv = w0 * pltpu.roll(u, 2, 0) + w1 * pltpu.roll(u, 1, 0) + w2 * u
    y_ref[...] = (b * conv).astype(BF16)
    head = jnp.concatenate([carry_ref[j], u[0:HEAD_ROWS]], axis=0)
    conv_head = w0 * pltpu.roll(head, 2, 0) + w1 * pltpu.roll(head, 1, 0) + w2 * head
    y_ref[0:HEAD_ROWS, :] = (b[0:HEAD_ROWS] * conv_head[SUBLANES:]).astype(BF16)
    tail = u[tm - SUBLANES:tm]
    carry_ref[j] = tail
    st_ref[0] = tail
    _cast_riders(rider_src, rider_dst)


def _conv_sample_kernel(x_ref, g_ref, wh_ref, wb_ref, wc_ref, cw_ref, s_ref, xn_ref, y_ref, st_ref,
                        *, n_seq, n_t):
    j = pl.program_id(0)

    @pl.when(j == 0)
    def _():
        xn_ref[...] = _rmsnorm(x_ref[...], g_ref[...]).astype(BF16)

    xn = xn_ref[...]
    h = _dot(xn, wh_ref[...])
    b = _dot(xn, wb_ref[...])
    c = _dot(xn, wc_ref[...])
    u = c * h
    w0 = cw_ref[0:1, :]
    w1 = cw_ref[1:2, :]
    w2 = cw_ref[2:3, :]
    n_state = CONV_WIDTH - 1
    ext = [s_ref[k * n_seq:(k + 1) * n_seq, :] for k in range(n_state)]
    ext += [u[t * n_seq:(t + 1) * n_seq] for t in range(n_t)]
    for t in range(n_t):
        conv = w0 * ext[t] + w1 * ext[t + 1] + w2 * ext[t + 2]
        y_ref[t * n_seq:(t + 1) * n_seq, :] = (b[t * n_seq:(t + 1) * n_seq] * conv).astype(BF16)
    for k in range(n_state):
        st_ref[k * n_seq:(k + 1) * n_seq, :] = ext[n_t + k]


def _conv_path_prompt(x2d, g, w_in_bf, conv_w, w_in, slab_weights, *, seq, d_conv, tm, tn):
    m, d = x2d.shape
    nct = d_conv // tn
    n_steps = (m // tm) * nct
    step_of = lambda i, j: i * nct + j
    riders = [_tail_rider(w_in, 3 * d_conv, n_steps, step_of, width=tn * 2)]
    riders += [_slab_rider(w, n_steps, step_of) for w in slab_weights]
    kern = functools.partial(_conv_prompt_kernel, tm=tm, blocks_per_seq=seq // tm, n_riders=len(riders))
    return pl.pallas_call(
        kern,
        grid=(m // tm, nct),
        in_specs=[
            pl.BlockSpec((tm, d), lambda i, j: (i, 0)),
            pl.BlockSpec((1, d), lambda i, j: (0, 0)),
            pl.BlockSpec((d, tn), lambda i, j: (0, j)),
            pl.BlockSpec((d, tn), lambda i, j: (0, nct + j)),
            pl.BlockSpec((d, tn), lambda i, j: (0, 2 * nct + j)),
            pl.BlockSpec((CONV_WIDTH, tn), lambda i, j: (0, j)),
        ] + [r[0] for r in riders],
        out_specs=[
            pl.BlockSpec((tm, d), lambda i, j: (i, 0)),
            pl.BlockSpec((tm, tn), lambda i, j: (i, j)),
            pl.BlockSpec((1, SUBLANES, tn), lambda i, j: (i, 0, j)),
        ] + [r[1] for r in riders],
        out_shape=[
            jax.ShapeDtypeStruct((m, d), BF16),
            jax.ShapeDtypeStruct((m, d_conv), BF16),
            jax.ShapeDtypeStruct((m // tm, SUBLANES, d_conv), F32),
        ] + [r[2] for r in riders],
        scratch_shapes=[pltpu.VMEM((nct, SUBLANES, tn), F32)],
        compiler_params=_params(2),
        name="conv_path_prompt",
    )(x2d, g, w_in_bf, w_in_bf, w_in_bf, conv_w, w_in, *slab_weights)


def _conv_path_sample(x2d, g, w_in_bf, conv_w, state_tm, *, n_seq, n_t, d_conv, tn):
    m, d = x2d.shape
    nct = d_conv // tn
    n_state = CONV_WIDTH - 1
    kern = functools.partial(_conv_sample_kernel, n_seq=n_seq, n_t=n_t)
    return pl.pallas_call(
        kern,
        grid=(nct,),
        in_specs=[
            pl.BlockSpec((m, d), lambda j: (0, 0)),
            pl.BlockSpec((1, d), lambda j: (0, 0)),
            pl.BlockSpec((d, tn), lambda j: (0, j)),
            pl.BlockSpec((d, tn), lambda j: (0, nct + j)),
            pl.BlockSpec((d, tn), lambda j: (0, 2 * nct + j)),
            pl.BlockSpec((CONV_WIDTH, tn), lambda j: (0, j)),
            pl.BlockSpec((n_state * n_seq, tn), lambda j: (0, j)),
        ],
        out_specs=[
            pl.BlockSpec((m, d), lambda j: (0, 0)),
            pl.BlockSpec((m, tn), lambda j: (0, j)),
            pl.BlockSpec((n_state * n_seq, tn), lambda j: (0, j)),
        ],
        out_shape=[
            jax.ShapeDtypeStruct((m, d), BF16),
            jax.ShapeDtypeStruct((m, d_conv), BF16),
            jax.ShapeDtypeStruct((n_state * n_seq, d_conv), F32),
        ],
        compiler_params=_params(1),
        name="conv_path_sample",
    )(x2d, g, w_in_bf, w_in_bf, w_in_bf, conv_w, state_tm)


def _pool_prompt_kernel(xn_ref, wv_ref, wp_ref, ps_ref, *rest, tm, blocks_per_seq, d_group, n_riders):
    rider_src = rest[:n_riders]
    mixed_ref, st_ref = rest[n_riders:n_riders + 2]
    rider_dst = rest[n_riders + 2:2 * n_riders + 2]
    carry_ref = rest[2 * n_riders + 2]
    i = pl.program_id(0)
    blk = i % blocks_per_seq

    @pl.when(blk == 0)
    def _():
        carry_ref[...] = jnp.zeros(carry_ref.shape, F32)

    xn = xn_ref[...]
    pos = blk * tm + lax.broadcasted_iota(jnp.int32, (tm, 1), 0)
    n_groups = len(POOL_WINDOWS)
    for g in sorted(range(n_groups), key=lambda q: (-(q // 2), q)):
        w = POOL_WINDOWS[g]
        lo, hi = g * d_group, (g + 1) * d_group
        if g % 2 == 0:
            v_pair = _dot(xn, wv_ref[:, lo:lo + 2 * d_group])
        v = v_pair[:, (g % 2) * d_group:(g % 2 + 1) * d_group]
        s = jnp.concatenate([carry_ref[:, lo:hi], v], axis=0)
        tail = v[tm - POOL_CARRY_ROWS:tm]
        carry_ref[:, lo:hi] = tail
        st_ref[0, :, lo:hi] = tail
        k = 1
        while k < w:
            s = s + pltpu.roll(s, k, 0)
            k *= 2
        cnt = jnp.minimum(pos + 1, w).astype(F32)
        pooled = s[POOL_CARRY_ROWS:] / cnt - v
        mixed = _dot(pooled.astype(BF16), wp_ref[g].astype(BF16)) * ps_ref[:, lo:hi]
        mixed_ref[:, lo:hi] = mixed.astype(BF16)
    _cast_riders(rider_src, rider_dst)


def _pool_sample_kernel(xn_ref, wv_ref, wp_ref, ps_ref, s_ref, mixed_ref, st_ref, *, n_seq, n_t, d_group):
    v = _dot(xn_ref[...], wv_ref[...])
    for g, w in enumerate(POOL_WINDOWS):
        lo, hi = g * d_group, (g + 1) * d_group
        ext = [s_ref[k * n_seq:(k + 1) * n_seq, lo:hi] for k in range(POOL_BUF)]
        ext += [v[t * n_seq:(t + 1) * n_seq, lo:hi] for t in range(n_t)]
        cnt = float(min(PAST_LEN + 1, w))
        wp = wp_ref[g].astype(BF16)
        for t in range(n_t):
            s = ext[POOL_BUF + t]
            for k in range(1, w):
                s = s + ext[POOL_BUF + t - k]
            pooled = s / cnt - ext[POOL_BUF + t]
            mixed = _dot(pooled.astype(BF16), wp) * ps_ref[:, lo:hi]
            mixed_ref[t * n_seq:(t + 1) * n_seq, lo:hi] = mixed.astype(BF16)
        for k in range(POOL_BUF):
            st_ref[k * n_seq:(k + 1) * n_seq, lo:hi] = ext[n_t + k]


def _pool_path_prompt(xn, w_tail_bf, w_pool, pool_scale, slab_weights, *, n_batch, seq, d_pool, tm):
    m, d = xn.shape
    n_groups = len(POOL_WINDOWS)
    d_group = d_pool // n_groups
    riders = [_slab_rider(w, m // tm, lambda i: i) for w in slab_weights]
    kern = functools.partial(_pool_prompt_kernel, tm=tm, blocks_per_seq=seq // tm, d_group=d_group,
                             n_riders=len(riders))
    return pl.pallas_call(
        kern,
        grid=(m // tm,),
        in_specs=[
            pl.BlockSpec((tm, d), lambda i: (i, 0)),
            pl.BlockSpec((d, d_pool), lambda i: (0, 0)),
            pl.BlockSpec((n_groups, d_group, d_group), lambda i: (0, 0, 0)),
            pl.BlockSpec((1, d_pool), lambda i: (0, 0)),
        ] + [r[0] for r in riders],
        out_specs=[
            pl.BlockSpec((tm, d_pool), lambda i: (i, 0)),
            pl.BlockSpec((1, POOL_CARRY_ROWS, d_pool), lambda i: ((i * tm) // seq, 0, 0)),
        ] + [r[1] for r in riders],
        out_shape=[
            jax.ShapeDtypeStruct((m, d_pool), BF16),
            jax.ShapeDtypeStruct((n_batch, POOL_CARRY_ROWS, d_pool), F32),
        ] + [r[2] for r in riders],
        scratch_shapes=[pltpu.VMEM((POOL_CARRY_ROWS, d_pool), F32)],
        compiler_params=_params(1),
        name="pool_path_prompt",
    )(xn, w_tail_bf, w_pool, pool_scale, *slab_weights)


def _pool_path_sample(xn, w_tail_bf, w_pool, pool_scale, state_tm, *, n_seq, n_t, d_pool):
    m, d = xn.shape
    n_groups = len(POOL_WINDOWS)
    d_group = d_pool // n_groups
    kern = functools.partial(_pool_sample_kernel, n_seq=n_seq, n_t=n_t, d_group=d_group)
    return pl.pallas_call(
        kern,
        grid=(1,),
        in_specs=[
            pl.BlockSpec((m, d), lambda i: (0, 0)),
            pl.BlockSpec((d, d_pool), lambda i: (0, 0)),
            pl.BlockSpec((n_groups, d_group, d_group), lambda i: (0, 0, 0)),
            pl.BlockSpec((1, d_pool), lambda i: (0, 0)),
            pl.BlockSpec((POOL_BUF * n_seq, d_pool), lambda i: (0, 0)),
        ],
        out_specs=[
            pl.BlockSpec((m, d_pool), lambda i: (0, 0)),
            pl.BlockSpec((POOL_BUF * n_seq, d_pool), lambda i: (0, 0)),
        ],
        out_shape=[
            jax.ShapeDtypeStruct((m, d_pool), BF16),
            jax.ShapeDtypeStruct((POOL_BUF * n_seq, d_pool), F32),
        ],
        compiler_params=_params(1),
        name="pool_path_sample",
    )(xn, w_tail_bf, w_pool, pool_scale, state_tm)


def _gate_merge_kernel(xn_ref, yc_ref, mx_ref, wgc_ref, wgp_ref, wbc_ref, wbp_ref, *rest, n_riders):
    rider_src = rest[:n_riders]
    o_ref = rest[n_riders]
    rider_dst = rest[n_riders + 1:2 * n_riders + 1]
    xn = xn_ref[...]
    gc = _dot(xn, wgc_ref[...])
    gp = _dot(xn, wgp_ref[...])
    y_conv = _dot(yc_ref[...], wbc_ref[...])
    y_pool = _dot(mx_ref[...], wbp_ref[...])
    o_ref[...] = (jax.nn.sigmoid(gc) * y_conv + jax.nn.sigmoid(gp) * y_pool).astype(BF16)
    _cast_riders(rider_src, rider_dst)


def _gate_merge(xn, ycin, mixed, w_tail_bf, w_brc_bf, w_brp_bf, slab_weights, *, gc_off, gp_off, tm, tn):
    m, d = xn.shape
    d_conv = ycin.shape[1]
    d_pool = mixed.shape[1]
    nct = d // tn
    riders = [_slab_rider(w, (m // tm) * nct, lambda i, j: i * nct + j) for w in slab_weights]
    out = pl.pallas_call(
        functools.partial(_gate_merge_kernel, n_riders=len(riders)),
        grid=(m // tm, nct),
        in_specs=[
            pl.BlockSpec((tm, d), lambda i, j: (i, 0)),
            pl.BlockSpec((tm, d_conv), lambda i, j: (i, 0)),
            pl.BlockSpec((tm, d_pool), lambda i, j: (i, 0)),
            pl.BlockSpec((d, tn), lambda i, j: (0, gc_off // tn + j)),
            pl.BlockSpec((d, tn), lambda i, j: (0, gp_off // tn + j)),
            pl.BlockSpec((d_conv, tn), lambda i, j: (0, j)),
            pl.BlockSpec((d_pool, tn), lambda i, j: (0, j)),
        ] + [r[0] for r in riders],
        out_specs=[pl.BlockSpec((tm, tn), lambda i, j: (i, j))] + [r[1] for r in riders],
        out_shape=[jax.ShapeDtypeStruct((m, d), BF16)] + [r[2] for r in riders],
        compiler_params=_params(2),
        name="gate_merge",
    )(xn, ycin, mixed, w_tail_bf, w_tail_bf, w_brc_bf, w_brp_bf, *slab_weights)
    return out[0], out[1:]


def _out_proj_kernel(x_ref, mg_ref, wo_ref, x1_ref):
    x1_ref[...] = x_ref[...] + _dot(mg_ref[...], wo_ref[...])


def _out_proj(x2d, merged, w_o_bf, *, tm, tn):
    m, d = x2d.shape
    return pl.pallas_call(
        _out_proj_kernel,
        grid=(d // tn, m // tm),
        in_specs=[
            pl.BlockSpec((tm, tn), lambda j, i: (i, j)),
            pl.BlockSpec((tm, d), lambda j, i: (i, 0)),
            pl.BlockSpec((d, tn), lambda j, i: (0, j)),
        ],
        out_specs=pl.BlockSpec((tm, tn), lambda j, i: (i, j)),
        out_shape=jax.ShapeDtypeStruct((m, d), F32),
        compiler_params=_params(2),
        name="out_proj_residual",
    )(x2d, merged, w_o_bf)


def _ffn_kernel(x1_ref, wg_ref, wu_ref, wd_ref, gf_ref, gl_ref, o_ref, hn_ref, *, final_norm):
    f = pl.program_id(1)

    @pl.when(f == 0)
    def _():
        x1 = x1_ref[...]
        hn_ref[...] = _rmsnorm(x1, gf_ref[...]).astype(BF16)
        o_ref[...] = x1

    hn = hn_ref[...]
    gate = _dot(hn, wg_ref[...])
    up = _dot(hn, wu_ref[...])
    ff = (gate * jax.nn.sigmoid(gate)) * up
    o_ref[...] += _dot(ff.astype(BF16), wd_ref[...])

    if final_norm:
        @pl.when(f == pl.num_programs(1) - 1)
        def _():
            o_ref[...] = _rmsnorm(o_ref[...], gl_ref[...])


def _ffn(x1, w_gate_bf, w_up_bf, w_down_bf, g_ffn, g_final, *, final_norm, tm, tf):
    m, d = x1.shape
    d_ff = w_gate_bf.shape[1]
    kern = functools.partial(_ffn_kernel, final_norm=final_norm)
    return pl.pallas_call(
        kern,
        grid=(m // tm, d_ff // tf),
        in_specs=[
            pl.BlockSpec((tm, d), lambda i, f: (i, 0)),
            pl.BlockSpec((d, tf), lambda i, f: (0, f)),
            pl.BlockSpec((d, tf), lambda i, f: (0, f)),
            pl.BlockSpec((tf, d), lambda i, f: (f, 0)),
            pl.BlockSpec((1, d), lambda i, f: (0, 0)),
            pl.BlockSpec((1, d), lambda i, f: (0, 0)),
        ],
        out_specs=pl.BlockSpec((tm, d), lambda i, f: (i, 0)),
        out_shape=jax.ShapeDtypeStruct((m, d), F32),
        scratch_shapes=[pltpu.VMEM((tm, d), BF16)],
        compiler_params=_params(2),
        name="swiglu_ffn",
    )(x1, w_gate_bf, w_up_bf, w_down_bf, g_ffn, g_final)


def _tiles(d_conv, d_ff):
    tm = 1024
    tm_pool = 1024
    tn_conv = 512 if d_conv % 512 == 0 else 256
    tn_merge = 512
    tn_out = 1024
    tf = 512 if d_ff % 512 == 0 else 256
    return tm, tm_pool, tn_conv, tn_merge, tn_out, tf


def kernel(x_prompt, x_sample, state_conv, state_pool, norm_mix, w_in, conv_w, w_pool, pool_scale, w_br_conv,
           w_br_pool, w_o, norm_ffn, w_gate, w_up, w_down, norm_final):
    n_batch, seq, d = x_prompt.shape
    n_seq, n_t, _ = x_sample.shape
    depth = w_in.shape[0]
    d_conv = conv_w.shape[2]
    d_pool = pool_scale.shape[1]
    d_ff = w_gate.shape[2]
    v_off = 3 * d_conv
    gc_off = d_pool
    gp_off = gc_off + d
    tm, tm_pool, tn_conv, tn_merge, tn_out, tf = _tiles(d_conv, d_ff)
    assert seq % tm == 0 and seq % tm_pool == 0 and tm_pool >= POOL_CARRY_ROWS and n_t >= CONV_WIDTH - 1
    assert n_seq % 16 == 0 and gc_off % tn_merge == 0 and gp_off % tn_merge == 0

    yp = x_prompt.reshape(n_batch * seq, d)
    ys = jnp.transpose(x_sample, (1, 0, 2)).reshape(n_t * n_seq, d)
    g_final = norm_final.reshape(1, d)
    conv_p, pool_p, conv_s, pool_s = [], [], [], []
    for l in range(depth):
        w_hbc_bf = w_in[l][:, :v_off].astype(BF16)
        g_mix = norm_mix[l].reshape(1, d)
        g_ffn = norm_ffn[l].reshape(1, d)
        ps = pool_scale[l].reshape(1, d_pool)
        last = l == depth - 1

        xn, ycin, st_c, w_tail_bf, w_brc_bf, w_brp_bf, w_o_bf = _conv_path_prompt(
            yp, g_mix, w_hbc_bf, conv_w[l], w_in[l], (w_br_conv[l], w_br_pool[l], w_o[l]),
            seq=seq, d_conv=d_conv, tm=tm, tn=tn_conv)
        mixed, st_p, w_down_bf = _pool_path_prompt(xn, w_tail_bf, w_pool[l], ps, (w_down[l],), n_batch=n_batch,
                                                   seq=seq, d_pool=d_pool, tm=tm_pool)
        merged, (w_gate_bf, w_up_bf) = _gate_merge(xn, ycin, mixed, w_tail_bf, w_brc_bf, w_brp_bf,
                                                   (w_gate[l], w_up[l]), gc_off=gc_off, gp_off=gp_off,
                                                   tm=tm, tn=tn_merge)
        x1 = _out_proj(yp, merged, w_o_bf, tm=tm, tn=tn_out)
        yp = _ffn(x1, w_gate_bf, w_up_bf, w_down_bf, g_ffn, g_final, final_norm=last, tm=tm, tf=tf)
        bps = seq // tm
        conv_p.append(st_c[bps - 1::bps, SUBLANES - (CONV_WIDTH - 1):, :])
        pool_p.append(st_p[:, POOL_CARRY_ROWS - POOL_BUF:, :])

        sc_tm = jnp.transpose(state_conv[l], (1, 0, 2)).reshape((CONV_WIDTH - 1) * n_seq, d_conv)
        sp_tm = jnp.transpose(state_pool[l], (1, 0, 2)).reshape(POOL_BUF * n_seq, d_pool)
        xn, ycin, st_c = _conv_path_sample(ys, g_mix, w_hbc_bf, conv_w[l], sc_tm, n_seq=n_seq, n_t=n_t,
                                           d_conv=d_conv, tn=tn_conv)
        mixed, st_p = _pool_path_sample(xn, w_tail_bf, w_pool[l], ps, sp_tm, n_seq=n_seq, n_t=n_t, d_pool=d_pool)
        merged, _ = _gate_merge(xn, ycin, mixed, w_tail_bf, w_brc_bf, w_brp_bf, (),
                                gc_off=gc_off, gp_off=gp_off, tm=n_t * n_seq, tn=tn_merge)
        x1 = _out_proj(ys, merged, w_o_bf, tm=n_t * n_seq, tn=tn_out)
        ys = _ffn(x1, w_gate_bf, w_up_bf, w_down_bf, g_ffn, g_final, final_norm=last, tm=n_t * n_seq, tf=tf)
        conv_s.append(jnp.transpose(st_c.reshape(CONV_WIDTH - 1, n_seq, d_conv), (1, 0, 2)))
        pool_s.append(jnp.transpose(st_p.reshape(POOL_BUF, n_seq, d_pool), (1, 0, 2)))

    y_prompt = yp.reshape(n_batch, seq, d)
    y_sample = jnp.transpose(ys.reshape(n_t, n_seq, d), (1, 0, 2))
    return (y_prompt, y_sample, jnp.stack(conv_p, axis=0), jnp.stack(pool_p, axis=0),
            jnp.stack(conv_s, axis=0), jnp.stack(pool_s, axis=0))
```

```python
import functools

import jax
import jax.numpy as jnp
from jax import lax
from jax.experimental import pallas as pl
from jax.experimental.pallas import tpu as pltpu

EPS = 1e-6
CONV_WIDTH = 3
POOL_WINDOWS = (2, 4, 8, 16)
POOL_BUF = max(POOL_WINDOWS) - 1
PAST_LEN = 16384

SUBLANES = 8
HEAD_ROWS = 16
POOL_CARRY_ROWS = 16
VMEM_LIMIT_BYTES = 56 * 1024 * 1024

BF16 = jnp.bfloat16
F32 = jnp.float32


def _dot(a, b):
    return jnp.dot(a, b, preferred_element_type=F32)


def _rmsnorm(x, g):
    r = lax.rsqrt(jnp.mean(x * x, axis=-1, keepdims=True) + EPS)
    return (x * r) * g


def _params(n_axes):
    return pltpu.CompilerParams(dimension_semantics=("arbitrary",) * n_axes,
                                vmem_limit_bytes=VMEM_LIMIT_BYTES)


def _cast_riders(src_refs, dst_refs):
    for src, dst in zip(src_refs, dst_refs):
        dst[...] = src[...].astype(BF16)


def _slab_rider(w, n_steps, step_of):
    rows, cols = w.shape
    slab = rows // n_steps
    assert slab * n_steps == rows and slab % HEAD_ROWS == 0
    spec = pl.BlockSpec((slab, cols), lambda *ids: (step_of(*ids), 0))
    return spec, spec, jax.ShapeDtypeStruct((rows, cols), BF16)


def _col_rider(w, col0, col1, n_steps, step_of, *, width):
    rows = w.shape[0]
    n_col = (col1 - col0) // width
    assert n_col * width == col1 - col0 and col0 % width == 0 and n_col <= n_steps
    n_row = 1
    while n_row * 2 * n_col <= n_steps and rows % (n_row * 2) == 0:
        n_row *= 2
    rb = rows // n_row
    assert rb % HEAD_ROWS == 0
    n_blk = n_row * n_col

    def blk(*ids):
        return jnp.minimum(step_of(*ids), n_blk - 1)

    in_spec = pl.BlockSpec((rb, width), lambda *ids: (blk(*ids) // n_col, col0 // width + blk(*ids) % n_col))
    out_spec = pl.BlockSpec((rb, width), lambda *ids: (blk(*ids) // n_col, blk(*ids) % n_col))
    return in_spec, out_spec, jax.ShapeDtypeStruct((rows, col1 - col0), BF16)


def _conv_prompt_kernel(xn_ref, wh_ref, wb_ref, wc_ref, cw_ref, *rest, tm, blocks_per_seq, n_riders):
    rider_src = rest[:n_riders]
    y_ref, st_ref = rest[n_riders:n_riders + 2]
    rider_dst = rest[n_riders + 2:2 * n_riders + 2]
    carry_ref = rest[2 * n_riders + 2]
    i = pl.program_id(0)
    j = pl.program_id(1)

    @pl.when(i % blocks_per_seq == 0)
    def _():
        carry_ref[j] = jnp.zeros(carry_ref.shape[1:], F32)

    xn = xn_ref[...]
    c = _dot(xn, wc_ref[...])
    h = _dot(xn, wh_ref[...])
    u = c * h
    b = _dot(xn, wb_ref[...])
    w0 = cw_ref[0:1, :]
    w1 = cw_ref[1:2, :]
    w2 = cw_ref[2:3, :]
    conv = w0 * pltpu.roll(u, 2, 0) + w1 * pltpu.roll(u, 1, 0) + w2 * u
    y_ref[...] = (b * conv).astype(BF16)
    head = jnp.concatenate([carry_ref[j], u[0:HEAD_ROWS]], axis=0)
    conv_head = w0 * pltpu.roll(head, 2, 0) + w1 * pltpu.roll(head, 1, 0) + w2 * head
    y_ref[0:HEAD_ROWS, :] = (b[0:HEAD_ROWS] * conv_head[SUBLANES:]).astype(BF16)
    tail = u[tm - SUBLANES:tm]
    carry_ref[j] = tail
    st_ref[0] = tail
    _cast_riders(rider_src, rider_dst)


def _conv_sample_kernel(x_ref, g_ref, wh_ref, wb_ref, wc_ref, cw_ref, s_ref, xn_ref, y_ref, st_ref,
                        *, n_seq, n_t):
    j = pl.program_id(0)

    @pl.when(j == 0)
    def _():
        xn_ref[...] = _rmsnorm(x_ref[...], g_ref[...]).astype(BF16)

    xn = xn_ref[...]
    h = _dot(xn, wh_ref[...])
    b = _dot(xn, wb_ref[...])
    c = _dot(xn, wc_ref[...])
    u = c * h
    w0 = cw_ref[0:1, :]
    w1 = cw_ref[1:2, :]
    w2 = cw_ref[2:3, :]
    n_state = CONV_WIDTH - 1
    ext = [s_ref[k * n_seq:(k + 1) * n_seq, :] for k in range(n_state)]
    ext += [u[t * n_seq:(t + 1) * n_seq] for t in range(n_t)]
    for t in range(n_t):
        conv = w0 * ext[t] + w1 * ext[t + 1] + w2 * ext[t + 2]
        y_ref[t * n_seq:(t + 1) * n_seq, :] = (b[t * n_seq:(t + 1) * n_seq] * conv).astype(BF16)
    for k in range(n_state):
        st_ref[k * n_seq:(k + 1) * n_seq, :] = ext[n_t + k]


def _conv_path_prompt(xn, w_hbc_bf, conv_w, w_in, slab_weights, *, seq, d_conv, tm, tn):
    m, d = xn.shape
    nct = d_conv // tn
    n_steps = (m // tm) * nct
    step_of = lambda i, j: i * nct + j
    riders = [_col_rider(w_in, 3 * d_conv, w_in.shape[1], n_steps, step_of, width=tn * 2)]
    riders += [_slab_rider(w, n_steps, step_of) for w in slab_weights]
    kern = functools.partial(_conv_prompt_kernel, tm=tm, blocks_per_seq=seq // tm, n_riders=len(riders))
    return pl.pallas_call(
        kern,
        grid=(m // tm, nct),
        in_specs=[
            pl.BlockSpec((tm, d), lambda i, j: (i, 0)),
            pl.BlockSpec((d, tn), lambda i, j: (0, j)),
            pl.BlockSpec((d, tn), lambda i, j: (0, nct + j)),
            pl.BlockSpec((d, tn), lambda i, j: (0, 2 * nct + j)),
            pl.BlockSpec((CONV_WIDTH, tn), lambda i, j: (0, j)),
        ] + [r[0] for r in riders],
        out_specs=[
            pl.BlockSpec((tm, tn), lambda i, j: (i, j)),
            pl.BlockSpec((1, SUBLANES, tn), lambda i, j: (i, 0, j)),
        ] + [r[1] for r in riders],
        out_shape=[
            jax.ShapeDtypeStruct((m, d_conv), BF16),
            jax.ShapeDtypeStruct((m // tm, SUBLANES, d_conv), F32),
        ] + [r[2] for r in riders],
        scratch_shapes=[pltpu.VMEM((nct, SUBLANES, tn), F32)],
        compiler_params=_params(2),
        name="conv_path_prompt",
    )(xn, w_hbc_bf, w_hbc_bf, w_hbc_bf, conv_w, w_in, *slab_weights)


def _conv_path_sample(x2d, g, w_in_bf, conv_w, state_tm, *, n_seq, n_t, d_conv, tn):
    m, d = x2d.shape
    nct = d_conv // tn
    n_state = CONV_WIDTH - 1
    kern = functools.partial(_conv_sample_kernel, n_seq=n_seq, n_t=n_t)
    return pl.pallas_call(
        kern,
        grid=(nct,),
        in_specs=[
            pl.BlockSpec((m, d), lambda j: (0, 0)),
            pl.BlockSpec((1, d), lambda j: (0, 0)),
            pl.BlockSpec((d, tn), lambda j: (0, j)),
            pl.BlockSpec((d, tn), lambda j: (0, nct + j)),
            pl.BlockSpec((d, tn), lambda j: (0, 2 * nct + j)),
            pl.BlockSpec((CONV_WIDTH, tn), lambda j: (0, j)),
            pl.BlockSpec((n_state * n_seq, tn), lambda j: (0, j)),
        ],
        out_specs=[
            pl.BlockSpec((m, d), lambda j: (0, 0)),
            pl.BlockSpec((m, tn), lambda j: (0, j)),
            pl.BlockSpec((n_state * n_seq, tn), lambda j: (0, j)),
        ],
        out_shape=[
            jax.ShapeDtypeStruct((m, d), BF16),
            jax.ShapeDtypeStruct((m, d_conv), BF16),
            jax.ShapeDtypeStruct((n_state * n_seq, d_conv), F32),
        ],
        compiler_params=_params(1),
        name="conv_path_sample",
    )(x2d, g, w_in_bf, w_in_bf, w_in_bf, conv_w, state_tm)


def _pool_prompt_kernel(x_ref, g_ref, wv_ref, wp_ref, ps_ref, *rest, tm, blocks_per_seq, d_group, n_riders):
    rider_src = rest[:n_riders]
    xn_ref, mixed_ref, st_ref = rest[n_riders:n_riders + 3]
    rider_dst = rest[n_riders + 3:2 * n_riders + 3]
    carry_ref, wvb_ref = rest[2 * n_riders + 3:2 * n_riders + 5]
    i = pl.program_id(0)
    blk = i % blocks_per_seq

    @pl.when(i == 0)
    def _():
        wvb_ref[...] = wv_ref[...].astype(BF16)

    @pl.when(blk == 0)
    def _():
        carry_ref[...] = jnp.zeros(carry_ref.shape, F32)

    xn = _rmsnorm(x_ref[...], g_ref[...]).astype(BF16)
    xn_ref[...] = xn
    pos = blk * tm + lax.broadcasted_iota(jnp.int32, (tm, 1), 0)
    n_groups = len(POOL_WINDOWS)
    for g in sorted(range(n_groups), key=lambda q: (-(q // 2), q)):
        w = POOL_WINDOWS[g]
        lo, hi = g * d_group, (g + 1) * d_group
        if g % 2 == 0:
            v_pair = _dot(xn, wvb_ref[:, lo:lo + 2 * d_group])
        v = v_pair[:, (g % 2) * d_group:(g % 2 + 1) * d_group]
        s = jnp.concatenate([carry_ref[:, lo:hi], v], axis=0)
        tail = v[tm - POOL_CARRY_ROWS:tm]
        carry_ref[:, lo:hi] = tail
        st_ref[0, :, lo:hi] = tail
        k = 1
        while k < w:
            s = s + pltpu.roll(s, k, 0)
            k *= 2
        cnt = jnp.minimum(pos + 1, w).astype(F32)
        pooled = s[POOL_CARRY_ROWS:] / cnt - v
        mixed = _dot(pooled.astype(BF16), wp_ref[g].astype(BF16)) * ps_ref[:, lo:hi]
        mixed_ref[:, lo:hi] = mixed.astype(BF16)
    _cast_riders(rider_src, rider_dst)


def _pool_sample_kernel(xn_ref, wv_ref, wp_ref, ps_ref, s_ref, mixed_ref, st_ref, *, n_seq, n_t, d_group):
    v = _dot(xn_ref[...], wv_ref[...])
    for g, w in enumerate(POOL_WINDOWS):
        lo, hi = g * d_group, (g + 1) * d_group
        ext = [s_ref[k * n_seq:(k + 1) * n_seq, lo:hi] for k in range(POOL_BUF)]
        ext += [v[t * n_seq:(t + 1) * n_seq, lo:hi] for t in range(n_t)]
        cnt = float(min(PAST_LEN + 1, w))
        wp = wp_ref[g].astype(BF16)
        for t in range(n_t):
            s = ext[POOL_BUF + t]
            for k in range(1, w):
                s = s + ext[POOL_BUF + t - k]
            pooled = s / cnt - ext[POOL_BUF + t]
            mixed = _dot(pooled.astype(BF16), wp) * ps_ref[:, lo:hi]
            mixed_ref[t * n_seq:(t + 1) * n_seq, lo:hi] = mixed.astype(BF16)
        for k in range(POOL_BUF):
            st_ref[k * n_seq:(k + 1) * n_seq, lo:hi] = ext[n_t + k]


def _pool_path_prompt(x2d, g, w_in, w_pool, pool_scale, *, n_batch, seq, d_conv, d_pool, tm):
    m, d = x2d.shape
    n_groups = len(POOL_WINDOWS)
    d_group = d_pool // n_groups
    v_off = 3 * d_conv
    riders = [_col_rider(w_in, 0, v_off, m // tm, lambda i: i, width=d_pool)]
    kern = functools.partial(_pool_prompt_kernel, tm=tm, blocks_per_seq=seq // tm, d_group=d_group,
                             n_riders=len(riders))
    return pl.pallas_call(
        kern,
        grid=(m // tm,),
        in_specs=[
            pl.BlockSpec((tm, d), lambda i: (i, 0)),
            pl.BlockSpec((1, d), lambda i: (0, 0)),
            pl.BlockSpec((d, d_pool), lambda i: (0, v_off // d_pool)),
            pl.BlockSpec((n_groups, d_group, d_group), lambda i: (0, 0, 0)),
            pl.BlockSpec((1, d_pool), lambda i: (0, 0)),
        ] + [r[0] for r in riders],
        out_specs=[
            pl.BlockSpec((tm, d), lambda i: (i, 0)),
            pl.BlockSpec((tm, d_pool), lambda i: (i, 0)),
            pl.BlockSpec((1, POOL_CARRY_ROWS, d_pool), lambda i: ((i * tm) // seq, 0, 0)),
        ] + [r[1] for r in riders],
        out_shape=[
            jax.ShapeDtypeStruct((m, d), BF16),
            jax.ShapeDtypeStruct((m, d_pool), BF16),
            jax.ShapeDtypeStruct((n_batch, POOL_CARRY_ROWS, d_pool), F32),
        ] + [r[2] for r in riders],
        scratch_shapes=[pltpu.VMEM((POOL_CARRY_ROWS, d_pool), F32), pltpu.VMEM((d, d_pool), BF16)],
        compiler_params=_params(1),
        name="pool_path_prompt",
    )(x2d, g, w_in, w_pool, pool_scale, w_in)


def _pool_path_sample(xn, w_tail_bf, w_pool, pool_scale, state_tm, *, n_seq, n_t, d_pool):
    m, d = xn.shape
    n_groups = len(POOL_WINDOWS)
    d_group = d_pool // n_groups
    kern = functools.partial(_pool_sample_kernel, n_seq=n_seq, n_t=n_t, d_group=d_group)
    return pl.pallas_call(
        kern,
        grid=(1,),
        in_specs=[
            pl.BlockSpec((m, d), lambda i: (0, 0)),
            pl.BlockSpec((d, d_pool), lambda i: (0, 0)),
            pl.BlockSpec((n_groups, d_group, d_group), lambda i: (0, 0, 0)),
            pl.BlockSpec((1, d_pool), lambda i: (0, 0)),
            pl.BlockSpec((POOL_BUF * n_seq, d_pool), lambda i: (0, 0)),
        ],
        out_specs=[
            pl.BlockSpec((m, d_pool), lambda i: (0, 0)),
            pl.BlockSpec((POOL_BUF * n_seq, d_pool), lambda i: (0, 0)),
        ],
        out_shape=[
            jax.ShapeDtypeStruct((m, d_pool), BF16),
            jax.ShapeDtypeStruct((POOL_BUF * n_seq, d_pool), F32),
        ],
        compiler_params=_params(1),
        name="pool_path_sample",
    )(xn, w_tail_bf, w_pool, pool_scale, state_tm)


def _gate_merge_kernel(xn_ref, yc_ref, mx_ref, wgc_ref, wgp_ref, wbc_ref, wbp_ref, *rest, n_riders):
    rider_src = rest[:n_riders]
    o_ref = rest[n_riders]
    rider_dst = rest[n_riders + 1:2 * n_riders + 1]
    xn = xn_ref[...]
    gc = _dot(xn, wgc_ref[...])
    gp = _dot(xn, wgp_ref[...])
    y_conv = _dot(yc_ref[...], wbc_ref[...])
    y_pool = _dot(mx_ref[...], wbp_ref[...])
    o_ref[...] = (jax.nn.sigmoid(gc) * y_conv + jax.nn.sigmoid(gp) * y_pool).astype(BF16)
    _cast_riders(rider_src, rider_dst)


def _gate_merge(xn, ycin, mixed, w_tail_bf, w_brc_bf, w_brp_bf, slab_weights, *, gc_off, gp_off, tm, tn):
    m, d = xn.shape
    d_conv = ycin.shape[1]
    d_pool = mixed.shape[1]
    nct = d // tn
    riders = [_slab_rider(w, (m // tm) * nct, lambda i, j: i * nct + j) for w in slab_weights]
    out = pl.pallas_call(
        functools.partial(_gate_merge_kernel, n_riders=len(riders)),
        grid=(m // tm, nct),
        in_specs=[
            pl.BlockSpec((tm, d), lambda i, j: (i, 0)),
            pl.BlockSpec((tm, d_conv), lambda i, j: (i, 0)),
            pl.BlockSpec((tm, d_pool), lambda i, j: (i, 0)),
            pl.BlockSpec((d, tn), lambda i, j: (0, gc_off // tn + j)),
            pl.BlockSpec((d, tn), lambda i, j: (0, gp_off // tn + j)),
            pl.BlockSpec((d_conv, tn), lambda i, j: (0, j)),
            pl.BlockSpec((d_pool, tn), lambda i, j: (0, j)),
        ] + [r[0] for r in riders],
        out_specs=[pl.BlockSpec((tm, tn), lambda i, j: (i, j))] + [r[1] for r in riders],
        out_shape=[jax.ShapeDtypeStruct((m, d), BF16)] + [r[2] for r in riders],
        compiler_params=_params(2),
        name="gate_merge",
    )(xn, ycin, mixed, w_tail_bf, w_tail_bf, w_brc_bf, w_brp_bf, *slab_weights)
    return out[0], out[1:]


def _out_proj_kernel(x_ref, mg_ref, wo_ref, x1_ref):
    x1_ref[...] = x_ref[...] + _dot(mg_ref[...], wo_ref[...])


def _out_proj(x2d, merged, w_o_bf, *, tm, tn):
    m, d = x2d.shape
    return pl.pallas_call(
        _out_proj_kernel,
        grid=(d // tn, m // tm),
        in_specs=[
            pl.BlockSpec((tm, tn), lambda j, i: (i, j)),
            pl.BlockSpec((tm, d), lambda j, i: (i, 0)),
            pl.BlockSpec((d, tn), lambda j, i: (0, j)),
        ],
        out_specs=pl.BlockSpec((tm, tn), lambda j, i: (i, j)),
        out_shape=jax.ShapeDtypeStruct((m, d), F32),
        compiler_params=_params(2),
        name="out_proj_residual",
    )(x2d, merged, w_o_bf)


def _ffn_kernel(x1_ref, wg_ref, wu_ref, wd_ref, gf_ref, gl_ref, o_ref, hn_ref, *, final_norm):
    f = pl.program_id(1)

    @pl.when(f == 0)
    def _():
        x1 = x1_ref[...]
        hn_ref[...] = _rmsnorm(x1, gf_ref[...]).astype(BF16)
        o_ref[...] = x1

    hn = hn_ref[...]
    gate = _dot(hn, wg_ref[...])
    up = _dot(hn, wu_ref[...])
    ff = (gate * jax.nn.sigmoid(gate)) * up
    o_ref[...] += _dot(ff.astype(BF16), wd_ref[...])

    if final_norm:
        @pl.when(f == pl.num_programs(1) - 1)
        def _():
            o_ref[...] = _rmsnorm(o_ref[...], gl_ref[...])


def _ffn(x1, w_gate_bf, w_up_bf, w_down_bf, g_ffn, g_final, *, final_norm, tm, tf):
    m, d = x1.shape
    d_ff = w_gate_bf.shape[1]
    kern = functools.partial(_ffn_kernel, final_norm=final_norm)
    return pl.pallas_call(
        kern,
        grid=(m // tm, d_ff // tf),
        in_specs=[
            pl.BlockSpec((tm, d), lambda i, f: (i, 0)),
            pl.BlockSpec((d, tf), lambda i, f: (0, f)),
            pl.BlockSpec((d, tf), lambda i, f: (0, f)),
            pl.BlockSpec((tf, d), lambda i, f: (f, 0)),
            pl.BlockSpec((1, d), lambda i, f: (0, 0)),
            pl.BlockSpec((1, d), lambda i, f: (0, 0)),
        ],
        out_specs=pl.BlockSpec((tm, d), lambda i, f: (i, 0)),
        out_shape=jax.ShapeDtypeStruct((m, d), F32),
        scratch_shapes=[pltpu.VMEM((tm, d), BF16)],
        compiler_params=_params(2),
        name="swiglu_ffn",
    )(x1, w_gate_bf, w_up_bf, w_down_bf, g_ffn, g_final)


def _tiles(d_conv, d_ff):
    tm = 1024
    tm_pool = 512
    tn_conv = 512 if d_conv % 512 == 0 else 256
    tn_merge = 512
    tn_out = 1024
    tf = 512 if d_ff % 512 == 0 else 256
    return tm, tm_pool, tn_conv, tn_merge, tn_out, tf


def kernel(x_prompt, x_sample, state_conv, state_pool, norm_mix, w_in, conv_w, w_pool, pool_scale, w_br_conv,
           w_br_pool, w_o, norm_ffn, w_gate, w_up, w_down, norm_final):
    n_batch, seq, d = x_prompt.shape
    n_seq, n_t, _ = x_sample.shape
    depth = w_in.shape[0]
    d_conv = conv_w.shape[2]
    d_pool = pool_scale.shape[1]
    d_ff = w_gate.shape[2]
    gc_off = d_pool
    gp_off = gc_off + d
    tm, tm_pool, tn_conv, tn_merge, tn_out, tf = _tiles(d_conv, d_ff)
    assert seq % tm == 0 and seq % tm_pool == 0 and tm_pool >= POOL_CARRY_ROWS and n_t >= CONV_WIDTH - 1
    assert n_seq % 16 == 0 and gc_off % tn_merge == 0 and gp_off % tn_merge == 0

    yp = x_prompt.reshape(n_batch * seq, d)
    ys = jnp.transpose(x_sample, (1, 0, 2)).reshape(n_t * n_seq, d)
    g_final = norm_final.reshape(1, d)
    conv_p, pool_p, conv_s, pool_s = [], [], [], []
    for l in range(depth):
        g_mix = norm_mix[l].reshape(1, d)
        g_ffn = norm_ffn[l].reshape(1, d)
        ps = pool_scale[l].reshape(1, d_pool)
        last = l == depth - 1

        xn, mixed, st_p, w_hbc_bf = _pool_path_prompt(yp, g_mix, w_in[l], w_pool[l], ps, n_batch=n_batch, seq=seq,
                                                      d_conv=d_conv, d_pool=d_pool, tm=tm_pool)
        ycin, st_c, w_tail_bf, w_brc_bf, w_brp_bf, w_o_bf, w_down_bf = _conv_path_prompt(
            xn, w_hbc_bf, conv_w[l], w_in[l], (w_br_conv[l], w_br_pool[l], w_o[l], w_down[l]),
            seq=seq, d_conv=d_conv, tm=tm, tn=tn_conv)
        merged, (w_gate_bf, w_up_bf) = _gate_merge(xn, ycin, mixed, w_tail_bf, w_brc_bf, w_brp_bf,
                                                   (w_gate[l], w_up[l]), gc_off=gc_off, gp_off=gp_off,
                                                   tm=tm, tn=tn_merge)
        x1 = _out_proj(yp, merged, w_o_bf, tm=tm, tn=tn_out)
        yp = _ffn(x1, w_gate_bf, w_up_bf, w_down_bf, g_ffn, g_final, final_norm=last, tm=tm, tf=tf)
        bps = seq // tm
        conv_p.append(st_c[bps - 1::bps, SUBLANES - (CONV_WIDTH - 1):, :])
        pool_p.append(st_p[:, POOL_CARRY_ROWS - POOL_BUF:, :])

        sc_tm = jnp.transpose(state_conv[l], (1, 0, 2)).reshape((CONV_WIDTH - 1) * n_seq, d_conv)
        sp_tm = jnp.transpose(state_pool[l], (1, 0, 2)).reshape(POOL_BUF * n_seq, d_pool)
        xn, ycin, st_c = _conv_path_sample(ys, g_mix, w_hbc_bf, conv_w[l], sc_tm, n_seq=n_seq, n_t=n_t,
                                           d_conv=d_conv, tn=tn_conv)
        mixed, st_p = _pool_path_sample(xn, w_tail_bf, w_pool[l], ps, sp_tm, n_seq=n_seq, n_t=n_t, d_pool=d_pool)
        merged, _ = _gate_merge(xn, ycin, mixed, w_tail_bf, w_brc_bf, w_brp_bf, (),
                                gc_off=gc_off, gp_off=gp_off, tm=n_t * n_seq, tn=tn_merge)
        x1 = _out_proj(ys, merged, w_o_bf, tm=n_t * n_seq, tn=tn_out)
        ys = _ffn(x1, w_gate_bf, w_up_bf, w_down_bf, g_ffn, g_final, final_norm=last, tm=n_t * n_seq, tf=tf)
        conv_s.append(jnp.transpose(st_c.reshape(CONV_WIDTH - 1, n_seq, d_conv), (1, 0, 2)))
        pool_s.append(jnp.transpose(st_p.reshape(POOL_BUF, n_seq, d_pool), (1, 0, 2)))

    y_prompt = yp.reshape(n_batch, seq, d)
    y_sample = jnp.transpose(ys.reshape(n_t, n_seq, d), (1, 0, 2))
    return (y_prompt, y_sample, jnp.stack(conv_p, axis=0), jnp.stack(pool_p, axis=0),
            jnp.stack(conv_s, axis=0), jnp.stack(pool_s, axis=0))
```

```python
import functools

import jax
import jax.numpy as jnp
from jax import lax
from jax.experimental import pallas as pl
from jax.experimental.pallas import tpu as pltpu

EPS = 1e-6
CONV_WIDTH = 3
POOL_WINDOWS = (2, 4, 8, 16)
POOL_BUF = max(POOL_WINDOWS) - 1
PAST_LEN = 16384

SUBLANES = 8
HEAD_ROWS = 16
POOL_CARRY_ROWS = 16
VMEM_LIMIT_BYTES = 56 * 1024 * 1024

BF16 = jnp.bfloat16
F32 = jnp.float32


def _dot(a, b):
    return jnp.dot(a, b, preferred_element_type=F32)


def _rmsnorm(x, g):
    r = lax.rsqrt(jnp.mean(x * x, axis=-1, keepdims=True) + EPS)
    return (x * r) * g


def _params(n_axes):
    return pltpu.CompilerParams(dimension_semantics=("arbitrary",) * n_axes,
                                vmem_limit_bytes=VMEM_LIMIT_BYTES)


def _cast_riders(src_refs, dst_refs):
    for src, dst in zip(src_refs, dst_refs):
        dst[...] = src[...].astype(BF16)


def _slab_rider(w, n_steps, step_of):
    rows, cols = w.shape
    slab = rows // n_steps
    assert slab * n_steps == rows and slab % HEAD_ROWS == 0
    spec = pl.BlockSpec((slab, cols), lambda *ids: (step_of(*ids), 0))
    return spec, spec, jax.ShapeDtypeStruct((rows, cols), BF16)


def _col_rider(w, col0, col1, n_steps, step_of, *, width):
    rows = w.shape[0]
    n_col = (col1 - col0) // width
    assert n_col * width == col1 - col0 and col0 % width == 0 and n_col <= n_steps
    n_row = 1
    while n_row * 2 * n_col <= n_steps and rows % (n_row * 2) == 0:
        n_row *= 2
    rb = rows // n_row
    assert rb % HEAD_ROWS == 0
    n_blk = n_row * n_col

    def blk(*ids):
        return jnp.minimum(step_of(*ids), n_blk - 1)

    in_spec = pl.BlockSpec((rb, width), lambda *ids: (blk(*ids) // n_col, col0 // width + blk(*ids) % n_col))
    out_spec = pl.BlockSpec((rb, width), lambda *ids: (blk(*ids) // n_col, blk(*ids) % n_col))
    return in_spec, out_spec, jax.ShapeDtypeStruct((rows, col1 - col0), BF16)


def _conv_prompt_kernel(xn_ref, wh_ref, wb_ref, wc_ref, cw_ref, *rest, tm, blocks_per_seq, n_riders):
    rider_src = rest[:n_riders]
    y_ref, st_ref = rest[n_riders:n_riders + 2]
    rider_dst = rest[n_riders + 2:2 * n_riders + 2]
    carry_ref = rest[2 * n_riders + 2]
    i = pl.program_id(0)
    j = pl.program_id(1)

    @pl.when(i % blocks_per_seq == 0)
    def _():
        carry_ref[j] = jnp.zeros(carry_ref.shape[1:], F32)

    xn = xn_ref[...]
    c = _dot(xn, wc_ref[...])
    h = _dot(xn, wh_ref[...])
    u = c * h
    b = _dot(xn, wb_ref[...])
    w0 = cw_ref[0:1, :]
    w1 = cw_ref[1:2, :]
    w2 = cw_ref[2:3, :]
    conv = w0 * pltpu.roll(u, 2, 0) + w1 * pltpu.roll(u, 1, 0) + w2 * u
    y_ref[...] = (b * conv).astype(BF16)
    head = jnp.concatenate([carry_ref[j], u[0:HEAD_ROWS]], axis=0)
    conv_head = w0 * pltpu.roll(head, 2, 0) + w1 * pltpu.roll(head, 1, 0) + w2 * head
    y_ref[0:HEAD_ROWS, :] = (b[0:HEAD_ROWS] * conv_head[SUBLANES:]).astype(BF16)
    tail = u[tm - SUBLANES:tm]
    carry_ref[j] = tail
    st_ref[0] = tail
    _cast_riders(rider_src, rider_dst)


def _conv_sample_kernel(*refs, n_seq, n_t):
    n_state = CONV_WIDTH - 1
    x_refs = refs[:n_t]
    g_ref, wh_ref, wb_ref, wc_ref, cw_ref = refs[n_t:n_t + 5]
    s_refs = refs[n_t + 5:n_t + 5 + n_state]
    xn_ref, y_ref = refs[n_t + 5 + n_state:n_t + 7 + n_state]
    st_refs = refs[n_t + 7 + n_state:]
    j = pl.program_id(0)

    @pl.when(j == 0)
    def _():
        for t in range(n_t):
            xn_ref[t * n_seq:(t + 1) * n_seq, :] = _rmsnorm(x_refs[t][...], g_ref[...]).astype(BF16)

    xn = xn_ref[...]
    c = _dot(xn, wc_ref[...])
    h = _dot(xn, wh_ref[...])
    u = c * h
    b = _dot(xn, wb_ref[...])
    w0 = cw_ref[0:1, :]
    w1 = cw_ref[1:2, :]
    w2 = cw_ref[2:3, :]
    ext = [r[...] for r in s_refs] + [u[t * n_seq:(t + 1) * n_seq] for t in range(n_t)]
    for t in range(n_t):
        conv = w0 * ext[t] + w1 * ext[t + 1] + w2 * ext[t + 2]
        y_ref[t * n_seq:(t + 1) * n_seq, :] = (b[t * n_seq:(t + 1) * n_seq] * conv).astype(BF16)
    for k in range(n_state):
        st_refs[k][...] = ext[n_t + k]


def _conv_path_prompt(xn, w_hbc_bf, conv_w, w_in, slab_weights, *, seq, d_conv, tm, tn):
    m, d = xn.shape
    nct = d_conv // tn
    n_steps = (m // tm) * nct
    step_of = lambda i, j: i * nct + j
    riders = [_col_rider(w_in, 3 * d_conv, w_in.shape[1], n_steps, step_of, width=tn * 2)]
    riders += [_slab_rider(w, n_steps, step_of) for w in slab_weights]
    kern = functools.partial(_conv_prompt_kernel, tm=tm, blocks_per_seq=seq // tm, n_riders=len(riders))
    return pl.pallas_call(
        kern,
        grid=(m // tm, nct),
        in_specs=[
            pl.BlockSpec((tm, d), lambda i, j: (i, 0)),
            pl.BlockSpec((d, tn), lambda i, j: (0, j)),
            pl.BlockSpec((d, tn), lambda i, j: (0, nct + j)),
            pl.BlockSpec((d, tn), lambda i, j: (0, 2 * nct + j)),
            pl.BlockSpec((CONV_WIDTH, tn), lambda i, j: (0, j)),
        ] + [r[0] for r in riders],
        out_specs=[
            pl.BlockSpec((tm, tn), lambda i, j: (i, j)),
            pl.BlockSpec((1, SUBLANES, tn), lambda i, j: (i, 0, j)),
        ] + [r[1] for r in riders],
        out_shape=[
            jax.ShapeDtypeStruct((m, d_conv), BF16),
            jax.ShapeDtypeStruct((m // tm, SUBLANES, d_conv), F32),
        ] + [r[2] for r in riders],
        scratch_shapes=[pltpu.VMEM((nct, SUBLANES, tn), F32)],
        compiler_params=_params(2),
        name="conv_path_prompt",
    )(xn, w_hbc_bf, w_hbc_bf, w_hbc_bf, conv_w, w_in, *slab_weights)


def _conv_path_sample(x_bt, g, w_in_bf, conv_w, state_bt, *, n_t, d_conv, tn):
    n_seq = x_bt.shape[0]
    d = x_bt.shape[1] // n_t
    m = n_t * n_seq
    nct = d_conv // tn
    n_state = CONV_WIDTH - 1
    kern = functools.partial(_conv_sample_kernel, n_seq=n_seq, n_t=n_t)
    return pl.pallas_call(
        kern,
        grid=(nct,),
        in_specs=[pl.BlockSpec((n_seq, d), lambda j, t=t: (0, t)) for t in range(n_t)] + [
            pl.BlockSpec((1, d), lambda j: (0, 0)),
            pl.BlockSpec((d, tn), lambda j: (0, j)),
            pl.BlockSpec((d, tn), lambda j: (0, nct + j)),
            pl.BlockSpec((d, tn), lambda j: (0, 2 * nct + j)),
            pl.BlockSpec((CONV_WIDTH, tn), lambda j: (0, j)),
        ] + [pl.BlockSpec((n_seq, tn), lambda j, k=k: (0, k * nct + j)) for k in range(n_state)],
        out_specs=[
            pl.BlockSpec((m, d), lambda j: (0, 0)),
            pl.BlockSpec((m, tn), lambda j: (0, j)),
        ] + [pl.BlockSpec((n_seq, tn), lambda j: (0, j)) for _ in range(n_state)],
        out_shape=[
            jax.ShapeDtypeStruct((m, d), BF16),
            jax.ShapeDtypeStruct((m, d_conv), BF16),
        ] + [jax.ShapeDtypeStruct((n_seq, d_conv), F32) for _ in range(n_state)],
        compiler_params=_params(1),
        name="conv_path_sample",
    )(*([x_bt] * n_t), g, w_in_bf, w_in_bf, w_in_bf, conv_w, *([state_bt] * n_state))


def _pool_prompt_kernel(x_ref, g_ref, wv_ref, wp_ref, ps_ref, *rest, tm, blocks_per_seq, d_group, n_riders):
    rider_src = rest[:n_riders]
    xn_ref, mixed_ref, st_ref = rest[n_riders:n_riders + 3]
    rider_dst = rest[n_riders + 3:2 * n_riders + 3]
    carry_ref, wvb_ref = rest[2 * n_riders + 3:2 * n_riders + 5]
    i = pl.program_id(0)
    blk = i % blocks_per_seq

    @pl.when(i == 0)
    def _():
        wvb_ref[...] = wv_ref[...].astype(BF16)

    @pl.when(blk == 0)
    def _():
        carry_ref[...] = jnp.zeros(carry_ref.shape, F32)

    xn = _rmsnorm(x_ref[...], g_ref[...]).astype(BF16)
    xn_ref[...] = xn
    pos = blk * tm + lax.broadcasted_iota(jnp.int32, (tm, 1), 0)
    n_groups = len(POOL_WINDOWS)
    for g in sorted(range(n_groups), key=lambda q: (-(q // 2), q)):
        w = POOL_WINDOWS[g]
        lo, hi = g * d_group, (g + 1) * d_group
        if g % 2 == 0:
            v_pair = _dot(xn, wvb_ref[:, lo:lo + 2 * d_group])
        v = v_pair[:, (g % 2) * d_group:(g % 2 + 1) * d_group]
        s = jnp.concatenate([carry_ref[:, lo:hi], v], axis=0)
        tail = v[tm - POOL_CARRY_ROWS:tm]
        carry_ref[:, lo:hi] = tail
        st_ref[0, :, lo:hi] = tail
        k = 1
        while k < w:
            s = s + pltpu.roll(s, k, 0)
            k *= 2
        cnt = jnp.minimum(pos + 1, w).astype(F32)
        pooled = s[POOL_CARRY_ROWS:] / cnt - v
        mixed = _dot(pooled.astype(BF16), wp_ref[g].astype(BF16)) * ps_ref[:, lo:hi]
        mixed_ref[:, lo:hi] = mixed.astype(BF16)
    _cast_riders(rider_src, rider_dst)


def _pool_sample_kernel(xn_ref, wv_ref, wp_ref, ps_ref, *rest, n_seq, n_t, d_group):
    s_refs = rest[:POOL_BUF]
    mixed_ref, st_ref = rest[POOL_BUF:]
    d_pool = len(POOL_WINDOWS) * d_group
    v = _dot(xn_ref[...], wv_ref[...])
    for g, w in enumerate(POOL_WINDOWS):
        lo, hi = g * d_group, (g + 1) * d_group
        ext = [r[:, lo:hi] for r in s_refs] + [v[t * n_seq:(t + 1) * n_seq, lo:hi] for t in range(n_t)]
        cnt = float(min(PAST_LEN + 1, w))
        wp = wp_ref[g].astype(BF16)
        for t in range(n_t):
            s = ext[POOL_BUF + t]
            for k in range(1, w):
                s = s + ext[POOL_BUF + t - k]
            pooled = s / cnt - ext[POOL_BUF + t]
            mixed = _dot(pooled.astype(BF16), wp) * ps_ref[:, lo:hi]
            mixed_ref[t * n_seq:(t + 1) * n_seq, lo:hi] = mixed.astype(BF16)
        for k in range(POOL_BUF):
            st_ref[:, k * d_pool + lo:k * d_pool + hi] = ext[n_t + k]


def _pool_path_prompt(x2d, g, w_in, w_pool, pool_scale, *, n_batch, seq, d_conv, d_pool, tm):
    m, d = x2d.shape
    n_groups = len(POOL_WINDOWS)
    d_group = d_pool // n_groups
    v_off = 3 * d_conv
    riders = [_col_rider(w_in, 0, v_off, m // tm, lambda i: i, width=d_pool)]
    kern = functools.partial(_pool_prompt_kernel, tm=tm, blocks_per_seq=seq // tm, d_group=d_group,
                             n_riders=len(riders))
    return pl.pallas_call(
        kern,
        grid=(m // tm,),
        in_specs=[
            pl.BlockSpec((tm, d), lambda i: (i, 0)),
            pl.BlockSpec((1, d), lambda i: (0, 0)),
            pl.BlockSpec((d, d_pool), lambda i: (0, v_off // d_pool)),
            pl.BlockSpec((n_groups, d_group, d_group), lambda i: (0, 0, 0)),
            pl.BlockSpec((1, d_pool), lambda i: (0, 0)),
        ] + [r[0] for r in riders],
        out_specs=[
            pl.BlockSpec((tm, d), lambda i: (i, 0)),
            pl.BlockSpec((tm, d_pool), lambda i: (i, 0)),
            pl.BlockSpec((1, POOL_CARRY_ROWS, d_pool), lambda i: ((i * tm) // seq, 0, 0)),
        ] + [r[1] for r in riders],
        out_shape=[
            jax.ShapeDtypeStruct((m, d), BF16),
            jax.ShapeDtypeStruct((m, d_pool), BF16),
            jax.ShapeDtypeStruct((n_batch, POOL_CARRY_ROWS, d_pool), F32),
        ] + [r[2] for r in riders],
        scratch_shapes=[pltpu.VMEM((POOL_CARRY_ROWS, d_pool), F32), pltpu.VMEM((d, d_pool), BF16)],
        compiler_params=_params(1),
        name="pool_path_prompt",
    )(x2d, g, w_in, w_pool, pool_scale, w_in)


def _pool_path_sample(xn, w_tail_bf, w_pool, pool_scale, state_bt, *, n_seq, n_t, d_pool):
    m, d = xn.shape
    n_groups = len(POOL_WINDOWS)
    d_group = d_pool // n_groups
    kern = functools.partial(_pool_sample_kernel, n_seq=n_seq, n_t=n_t, d_group=d_group)
    return pl.pallas_call(
        kern,
        grid=(1,),
        in_specs=[
            pl.BlockSpec((m, d), lambda i: (0, 0)),
            pl.BlockSpec((d, d_pool), lambda i: (0, 0)),
            pl.BlockSpec((n_groups, d_group, d_group), lambda i: (0, 0, 0)),
            pl.BlockSpec((1, d_pool), lambda i: (0, 0)),
        ] + [pl.BlockSpec((n_seq, d_pool), lambda i, k=k: (0, k)) for k in range(POOL_BUF)],
        out_specs=[
            pl.BlockSpec((m, d_pool), lambda i: (0, 0)),
            pl.BlockSpec((n_seq, POOL_BUF * d_pool), lambda i: (0, 0)),
        ],
        out_shape=[
            jax.ShapeDtypeStruct((m, d_pool), BF16),
            jax.ShapeDtypeStruct((n_seq, POOL_BUF * d_pool), F32),
        ],
        compiler_params=_params(1),
        name="pool_path_sample",
    )(xn, w_tail_bf, w_pool, pool_scale, *([state_bt] * POOL_BUF))


def _gate_merge_kernel(xn_ref, yc_ref, mx_ref, wgc_ref, wgp_ref, wbc_ref, wbp_ref, *rest, n_riders):
    rider_src = rest[:n_riders]
    o_ref = rest[n_riders]
    rider_dst = rest[n_riders + 1:2 * n_riders + 1]
    xn = xn_ref[...]
    gc = _dot(xn, wgc_ref[...])
    gp = _dot(xn, wgp_ref[...])
    y_conv = _dot(yc_ref[...], wbc_ref[...])
    y_pool = _dot(mx_ref[...], wbp_ref[...])
    o_ref[...] = (jax.nn.sigmoid(gc) * y_conv + jax.nn.sigmoid(gp) * y_pool).astype(BF16)
    _cast_riders(rider_src, rider_dst)


def _gate_merge(xn, ycin, mixed, w_tail_bf, w_brc_bf, w_brp_bf, slab_weights, *, gc_off, gp_off, tm, tn):
    m, d = xn.shape
    d_conv = ycin.shape[1]
    d_pool = mixed.shape[1]
    nct = d // tn
    riders = [_slab_rider(w, (m // tm) * nct, lambda i, j: i * nct + j) for w in slab_weights]
    out = pl.pallas_call(
        functools.partial(_gate_merge_kernel, n_riders=len(riders)),
        grid=(m // tm, nct),
        in_specs=[
            pl.BlockSpec((tm, d), lambda i, j: (i, 0)),
            pl.BlockSpec((tm, d_conv), lambda i, j: (i, 0)),
            pl.BlockSpec((tm, d_pool), lambda i, j: (i, 0)),
            pl.BlockSpec((d, tn), lambda i, j: (0, gc_off // tn + j)),
            pl.BlockSpec((d, tn), lambda i, j: (0, gp_off // tn + j)),
            pl.BlockSpec((d_conv, tn), lambda i, j: (0, j)),
            pl.BlockSpec((d_pool, tn), lambda i, j: (0, j)),
        ] + [r[0] for r in riders],
        out_specs=[pl.BlockSpec((tm, tn), lambda i, j: (i, j))] + [r[1] for r in riders],
        out_shape=[jax.ShapeDtypeStruct((m, d), BF16)] + [r[2] for r in riders],
        compiler_params=_params(2),
        name="gate_merge",
    )(xn, ycin, mixed, w_tail_bf, w_tail_bf, w_brc_bf, w_brp_bf, *slab_weights)
    return out[0], out[1:]


def _out_proj_kernel(*refs, n_slabs):
    x_refs = refs[:n_slabs]
    mg_ref, wo_ref, x1_ref = refs[n_slabs:]
    acc = _dot(mg_ref[...], wo_ref[...])
    rows = x1_ref.shape[0] // n_slabs
    for t in range(n_slabs):
        x1_ref[t * rows:(t + 1) * rows, :] = x_refs[t][...] + acc[t * rows:(t + 1) * rows]


def _out_proj(x2d, merged, w_o_bf, *, tm, tn, n_slabs=1):
    m, d = merged.shape
    nct = d // tn
    rows = tm // n_slabs
    assert n_slabs == 1 or tm == m
    return pl.pallas_call(
        functools.partial(_out_proj_kernel, n_slabs=n_slabs),
        grid=(nct, m // tm),
        in_specs=[pl.BlockSpec((rows, tn), lambda j, i, t=t: (i, t * nct + j)) for t in range(n_slabs)] + [
            pl.BlockSpec((tm, d), lambda j, i: (i, 0)),
            pl.BlockSpec((d, tn), lambda j, i: (0, j)),
        ],
        out_specs=pl.BlockSpec((tm, tn), lambda j, i: (i, j)),
        out_shape=jax.ShapeDtypeStruct((m, d), F32),
        compiler_params=_params(2),
        name="out_proj_residual",
    )(*([x2d] * n_slabs), merged, w_o_bf)


def _ffn_kernel(x1_ref, wg_ref, wu_ref, wd_ref, gf_ref, gl_ref, o_ref, hn_ref, *, final_norm, n_slabs):
    f = pl.program_id(1)
    d = x1_ref.shape[1]
    rows = x1_ref.shape[0] // n_slabs

    def o_slab(t):
        return (slice(None), slice(None)) if n_slabs == 1 else (slice(None), slice(t * d, (t + 1) * d))

    @pl.when(f == 0)
    def _():
        x1 = x1_ref[...]
        hn_ref[...] = _rmsnorm(x1, gf_ref[...]).astype(BF16)
        for t in range(n_slabs):
            o_ref[o_slab(t)] = x1[t * rows:(t + 1) * rows]

    hn = hn_ref[...]
    gate = _dot(hn, wg_ref[...])
    up = _dot(hn, wu_ref[...])
    ff = (gate * jax.nn.sigmoid(gate)) * up
    down = _dot(ff.astype(BF16), wd_ref[...])
    for t in range(n_slabs):
        o_ref[o_slab(t)] += down[t * rows:(t + 1) * rows]

    if final_norm:
        @pl.when(f == pl.num_programs(1) - 1)
        def _():
            for t in range(n_slabs):
                o_ref[o_slab(t)] = _rmsnorm(o_ref[o_slab(t)], gl_ref[...])


def _ffn(x1, w_gate_bf, w_up_bf, w_down_bf, g_ffn, g_final, *, final_norm, tm, tf, n_slabs=1):
    m, d = x1.shape
    d_ff = w_gate_bf.shape[1]
    assert n_slabs == 1 or tm == m
    kern = functools.partial(_ffn_kernel, final_norm=final_norm, n_slabs=n_slabs)
    return pl.pallas_call(
        kern,
        grid=(m // tm, d_ff // tf),
        in_specs=[
            pl.BlockSpec((tm, d), lambda i, f: (i, 0)),
            pl.BlockSpec((d, tf), lambda i, f: (0, f)),
            pl.BlockSpec((d, tf), lambda i, f: (0, f)),
            pl.BlockSpec((tf, d), lambda i, f: (f, 0)),
            pl.BlockSpec((1, d), lambda i, f: (0, 0)),
            pl.BlockSpec((1, d), lambda i, f: (0, 0)),
        ],
        out_specs=pl.BlockSpec((tm // n_slabs, n_slabs * d), lambda i, f: (i, 0)),
        out_shape=jax.ShapeDtypeStruct((m // n_slabs, n_slabs * d), F32),
        scratch_shapes=[pltpu.VMEM((tm, d), BF16)],
        compiler_params=_params(2),
        name="swiglu_ffn",
    )(x1, w_gate_bf, w_up_bf, w_down_bf, g_ffn, g_final)


def _tiles(d_conv, d_ff):
    tm = 1024
    tm_pool = 512
    tn_conv = 512 if d_conv % 512 == 0 else 256
    tn_merge = 512
    tn_out = 1024
    tf = 512 if d_ff % 512 == 0 else 256
    return tm, tm_pool, tn_conv, tn_merge, tn_out, tf


def kernel(x_prompt, x_sample, state_conv, state_pool, norm_mix, w_in, conv_w, w_pool, pool_scale, w_br_conv,
           w_br_pool, w_o, norm_ffn, w_gate, w_up, w_down, norm_final):
    n_batch, seq, d = x_prompt.shape
    n_seq, n_t, _ = x_sample.shape
    depth = w_in.shape[0]
    d_conv = conv_w.shape[2]
    d_pool = pool_scale.shape[1]
    d_ff = w_gate.shape[2]
    gc_off = d_pool
    gp_off = gc_off + d
    tm, tm_pool, tn_conv, tn_merge, tn_out, tf = _tiles(d_conv, d_ff)
    assert seq % tm == 0 and seq % tm_pool == 0 and tm_pool >= POOL_CARRY_ROWS and n_t >= CONV_WIDTH - 1
    assert n_seq % 16 == 0 and gc_off % tn_merge == 0 and gp_off % tn_merge == 0

    yp = x_prompt.reshape(n_batch * seq, d)
    ys = x_sample.reshape(n_seq, n_t * d)
    g_final = norm_final.reshape(1, d)
    conv_p, pool_p, conv_s, pool_s = [], [], [], []
    for l in range(depth):
        g_mix = norm_mix[l].reshape(1, d)
        g_ffn = norm_ffn[l].reshape(1, d)
        ps = pool_scale[l].reshape(1, d_pool)
        last = l == depth - 1

        xn, mixed, st_p, w_hbc_bf = _pool_path_prompt(yp, g_mix, w_in[l], w_pool[l], ps, n_batch=n_batch, seq=seq,
                                                      d_conv=d_conv, d_pool=d_pool, tm=tm_pool)
        ycin, st_c, w_tail_bf, w_brc_bf, w_brp_bf, w_o_bf, w_down_bf = _conv_path_prompt(
            xn, w_hbc_bf, conv_w[l], w_in[l], (w_br_conv[l], w_br_pool[l], w_o[l], w_down[l]),
            seq=seq, d_conv=d_conv, tm=tm, tn=tn_conv)
        merged, (w_gate_bf, w_up_bf) = _gate_merge(xn, ycin, mixed, w_tail_bf, w_brc_bf, w_brp_bf,
                                                   (w_gate[l], w_up[l]), gc_off=gc_off, gp_off=gp_off,
                                                   tm=tm, tn=tn_merge)
        x1 = _out_proj(yp, merged, w_o_bf, tm=tm, tn=tn_out)
        yp = _ffn(x1, w_gate_bf, w_up_bf, w_down_bf, g_ffn, g_final, final_norm=last, tm=tm, tf=tf)
        bps = seq // tm
        conv_p.append(st_c[bps - 1::bps, SUBLANES - (CONV_WIDTH - 1):, :])
        pool_p.append(st_p[:, POOL_CARRY_ROWS - POOL_BUF:, :])

        xn, ycin, *st_c = _conv_path_sample(ys, g_mix, w_hbc_bf, conv_w[l],
                                            state_conv[l].reshape(n_seq, (CONV_WIDTH - 1) * d_conv),
                                            n_t=n_t, d_conv=d_conv, tn=tn_conv)
        mixed, st_p = _pool_path_sample(xn, w_tail_bf, w_pool[l], ps, state_pool[l].reshape(n_seq, POOL_BUF * d_pool),
                                        n_seq=n_seq, n_t=n_t, d_pool=d_pool)
        merged, _ = _gate_merge(xn, ycin, mixed, w_tail_bf, w_brc_bf, w_brp_bf, (),
                                gc_off=gc_off, gp_off=gp_off, tm=n_t * n_seq, tn=tn_merge)
        x1 = _out_proj(ys, merged, w_o_bf, tm=n_t * n_seq, tn=tn_out, n_slabs=n_t)
        ys = _ffn(x1, w_gate_bf, w_up_bf, w_down_bf, g_ffn, g_final, final_norm=last, tm=n_t * n_seq, tf=tf,
                  n_slabs=n_t)
        conv_s.append(jnp.stack(st_c, axis=1))
        pool_s.append(st_p.reshape(n_seq, POOL_BUF, d_pool))

    y_prompt = yp.reshape(n_batch, seq, d)
    y_sample = ys.reshape(n_seq, n_t, d)
    return (y_prompt, y_sample, jnp.stack(conv_p, axis=0), jnp.stack(pool_p, axis=0),
            jnp.stack(conv_s, axis=0), jnp.stack(pool_s, axis=0))
```

```python
import functools

import jax
import jax.numpy as jnp
from jax import lax
from jax.experimental import pallas as pl
from jax.experimental.pallas import tpu as pltpu

EPS = 1e-6
CONV_WIDTH = 3
POOL_WINDOWS = (2, 4, 8, 16)
POOL_BUF = max(POOL_WINDOWS) - 1
PAST_LEN = 16384

SUBLANES = 8
HEAD_ROWS = 16
POOL_CARRY_ROWS = 16
VMEM_LIMIT_BYTES = 56 * 1024 * 1024

BF16 = jnp.bfloat16
F32 = jnp.float32


def _dot(a, b):
    return jnp.dot(a, b, preferred_element_type=F32)


def _rmsnorm(x, g):
    r = lax.rsqrt(jnp.mean(x * x, axis=-1, keepdims=True) + EPS)
    return (x * r) * g


def _params(n_axes):
    return pltpu.CompilerParams(dimension_semantics=("arbitrary",) * n_axes,
                                vmem_limit_bytes=VMEM_LIMIT_BYTES)


def _cast_riders(src_refs, dst_refs):
    for src, dst in zip(src_refs, dst_refs):
        dst[...] = src[...].astype(BF16)


def _slab_rider(w, n_steps, step_of):
    rows, cols = w.shape
    slab = rows // n_steps
    assert slab * n_steps == rows and slab % HEAD_ROWS == 0
    spec = pl.BlockSpec((slab, cols), lambda *ids: (step_of(*ids), 0))
    return spec, spec, jax.ShapeDtypeStruct((rows, cols), BF16)


def _col_rider(w, col0, col1, n_steps, step_of, *, width):
    rows = w.shape[0]
    n_col = (col1 - col0) // width
    assert n_col * width == col1 - col0 and col0 % width == 0 and n_col <= n_steps
    n_row = 1
    while n_row * 2 * n_col <= n_steps and rows % (n_row * 2) == 0:
        n_row *= 2
    rb = rows // n_row
    assert rb % HEAD_ROWS == 0
    n_blk = n_row * n_col

    def blk(*ids):
        return jnp.minimum(step_of(*ids), n_blk - 1)

    in_spec = pl.BlockSpec((rb, width), lambda *ids: (blk(*ids) // n_col, col0 // width + blk(*ids) % n_col))
    out_spec = pl.BlockSpec((rb, width), lambda *ids: (blk(*ids) // n_col, blk(*ids) % n_col))
    return in_spec, out_spec, jax.ShapeDtypeStruct((rows, col1 - col0), BF16)


def _conv_prompt_kernel(xn_ref, wh_ref, wb_ref, wc_ref, cw_ref, *rest, tm, blocks_per_seq, n_riders):
    rider_src = rest[:n_riders]
    y_ref, st_ref = rest[n_riders:n_riders + 2]
    rider_dst = rest[n_riders + 2:2 * n_riders + 2]
    carry_ref, whb_ref, wbb_ref, wcb_ref = rest[2 * n_riders + 2:]
    i = pl.program_id(1)

    @pl.when(i == 0)
    def _():
        whb_ref[...] = wh_ref[...].astype(BF16)
        wbb_ref[...] = wb_ref[...].astype(BF16)
        wcb_ref[...] = wc_ref[...].astype(BF16)

    @pl.when(i % blocks_per_seq == 0)
    def _():
        carry_ref[...] = jnp.zeros(carry_ref.shape, F32)

    xn = xn_ref[...]
    c = _dot(xn, wcb_ref[...])
    h = _dot(xn, whb_ref[...])
    u = c * h
    b = _dot(xn, wbb_ref[...])
    w0 = cw_ref[0:1, :]
    w1 = cw_ref[1:2, :]
    w2 = cw_ref[2:3, :]
    conv = w0 * pltpu.roll(u, 2, 0) + w1 * pltpu.roll(u, 1, 0) + w2 * u
    y_ref[...] = (b * conv).astype(BF16)
    head = jnp.concatenate([carry_ref[...], u[0:HEAD_ROWS]], axis=0)
    conv_head = w0 * pltpu.roll(head, 2, 0) + w1 * pltpu.roll(head, 1, 0) + w2 * head
    y_ref[0:HEAD_ROWS, :] = (b[0:HEAD_ROWS] * conv_head[SUBLANES:]).astype(BF16)
    tail = u[tm - SUBLANES:tm]
    carry_ref[...] = tail
    st_ref[0] = tail
    _cast_riders(rider_src, rider_dst)


def _conv_sample_kernel(x_ref, g_ref, wh_ref, wb_ref, wc_ref, cw_ref, s_ref, xn_ref, y_ref, st_ref,
                        *, n_seq, n_t):
    j = pl.program_id(0)

    @pl.when(j == 0)
    def _():
        xn_ref[...] = _rmsnorm(x_ref[...], g_ref[...]).astype(BF16)

    xn = xn_ref[...]
    c = _dot(xn, wc_ref[...].astype(BF16))
    h = _dot(xn, wh_ref[...].astype(BF16))
    u = c * h
    b = _dot(xn, wb_ref[...].astype(BF16))
    w0 = cw_ref[0:1, :]
    w1 = cw_ref[1:2, :]
    w2 = cw_ref[2:3, :]
    n_state = CONV_WIDTH - 1
    ext = [s_ref[k * n_seq:(k + 1) * n_seq, :] for k in range(n_state)]
    ext += [u[t * n_seq:(t + 1) * n_seq] for t in range(n_t)]
    for t in range(n_t):
        conv = w0 * ext[t] + w1 * ext[t + 1] + w2 * ext[t + 2]
        y_ref[t * n_seq:(t + 1) * n_seq, :] = (b[t * n_seq:(t + 1) * n_seq] * conv).astype(BF16)
    for k in range(n_state):
        st_ref[k * n_seq:(k + 1) * n_seq, :] = ext[n_t + k]


def _conv_path_prompt(xn, conv_w, w_in, slab_weights, *, seq, d_conv, tm, tn):
    m, d = xn.shape
    nct = d_conv // tn
    n_i = m // tm
    n_steps = nct * n_i
    step_of = lambda j, i: j * n_i + i
    riders = [_col_rider(w_in, 3 * d_conv, w_in.shape[1], n_steps, step_of, width=tn * 2)]
    riders += [_slab_rider(w, n_steps, step_of) for w in slab_weights]
    kern = functools.partial(_conv_prompt_kernel, tm=tm, blocks_per_seq=seq // tm, n_riders=len(riders))
    return pl.pallas_call(
        kern,
        grid=(nct, n_i),
        in_specs=[
            pl.BlockSpec((tm, d), lambda j, i: (i, 0)),
            pl.BlockSpec((d, tn), lambda j, i: (0, j)),
            pl.BlockSpec((d, tn), lambda j, i: (0, nct + j)),
            pl.BlockSpec((d, tn), lambda j, i: (0, 2 * nct + j)),
            pl.BlockSpec((CONV_WIDTH, tn), lambda j, i: (0, j)),
        ] + [r[0] for r in riders],
        out_specs=[
            pl.BlockSpec((tm, tn), lambda j, i: (i, j)),
            pl.BlockSpec((1, SUBLANES, tn), lambda j, i: (i, 0, j)),
        ] + [r[1] for r in riders],
        out_shape=[
            jax.ShapeDtypeStruct((m, d_conv), BF16),
            jax.ShapeDtypeStruct((n_i, SUBLANES, d_conv), F32),
        ] + [r[2] for r in riders],
        scratch_shapes=[pltpu.VMEM((SUBLANES, tn), F32)] + [pltpu.VMEM((d, tn), BF16)] * 3,
        compiler_params=_params(2),
        name="conv_path_prompt",
    )(xn, w_in, w_in, w_in, conv_w, w_in, *slab_weights)


def _conv_path_sample(x2d, g, w_in, conv_w, state_tm, *, n_seq, n_t, d_conv, tn):
    m, d = x2d.shape
    nct = d_conv // tn
    n_state = CONV_WIDTH - 1
    kern = functools.partial(_conv_sample_kernel, n_seq=n_seq, n_t=n_t)
    return pl.pallas_call(
        kern,
        grid=(nct,),
        in_specs=[
            pl.BlockSpec((m, d), lambda j: (0, 0)),
            pl.BlockSpec((1, d), lambda j: (0, 0)),
            pl.BlockSpec((d, tn), lambda j: (0, j)),
            pl.BlockSpec((d, tn), lambda j: (0, nct + j)),
            pl.BlockSpec((d, tn), lambda j: (0, 2 * nct + j)),
            pl.BlockSpec((CONV_WIDTH, tn), lambda j: (0, j)),
            pl.BlockSpec((n_state * n_seq, tn), lambda j: (0, j)),
        ],
        out_specs=[
            pl.BlockSpec((m, d), lambda j: (0, 0)),
            pl.BlockSpec((m, tn), lambda j: (0, j)),
            pl.BlockSpec((n_state * n_seq, tn), lambda j: (0, j)),
        ],
        out_shape=[
            jax.ShapeDtypeStruct((m, d), BF16),
            jax.ShapeDtypeStruct((m, d_conv), BF16),
            jax.ShapeDtypeStruct((n_state * n_seq, d_conv), F32),
        ],
        compiler_params=_params(1),
        name="conv_path_sample",
    )(x2d, g, w_in, w_in, w_in, conv_w, state_tm)


def _pool_prompt_kernel(x_ref, g_ref, wv_ref, wp_ref, ps_ref, xn_ref, mixed_ref, st_ref, carry_ref, wvb_ref,
                        *, tm, blocks_per_seq, d_group):
    i = pl.program_id(0)
    blk = i % blocks_per_seq

    @pl.when(i == 0)
    def _():
        wvb_ref[...] = wv_ref[...].astype(BF16)

    @pl.when(blk == 0)
    def _():
        carry_ref[...] = jnp.zeros(carry_ref.shape, F32)

    xn = _rmsnorm(x_ref[...], g_ref[...]).astype(BF16)
    xn_ref[...] = xn
    pos = blk * tm + lax.broadcasted_iota(jnp.int32, (tm, 1), 0)
    n_groups = len(POOL_WINDOWS)
    for g in sorted(range(n_groups), key=lambda q: (-(q // 2), q)):
        w = POOL_WINDOWS[g]
        lo, hi = g * d_group, (g + 1) * d_group
        if g % 2 == 0:
            v_pair = _dot(xn, wvb_ref[:, lo:lo + 2 * d_group])
        v = v_pair[:, (g % 2) * d_group:(g % 2 + 1) * d_group]
        s = jnp.concatenate([carry_ref[:, lo:hi], v], axis=0)
        tail = v[tm - POOL_CARRY_ROWS:tm]
        carry_ref[:, lo:hi] = tail
        st_ref[0, :, lo:hi] = tail
        k = 1
        while k < w:
            s = s + pltpu.roll(s, k, 0)
            k *= 2
        cnt = jnp.minimum(pos + 1, w).astype(F32)
        pooled = s[POOL_CARRY_ROWS:] / cnt - v
        mixed = _dot(pooled.astype(BF16), wp_ref[g].astype(BF16)) * ps_ref[:, lo:hi]
        mixed_ref[:, lo:hi] = mixed.astype(BF16)


def _pool_sample_kernel(xn_ref, wv_ref, wp_ref, ps_ref, s_ref, mixed_ref, st_ref, *, n_seq, n_t, d_group):
    v = _dot(xn_ref[...], wv_ref[...])
    for g, w in enumerate(POOL_WINDOWS):
        lo, hi = g * d_group, (g + 1) * d_group
        ext = [s_ref[k * n_seq:(k + 1) * n_seq, lo:hi] for k in range(POOL_BUF)]
        ext += [v[t * n_seq:(t + 1) * n_seq, lo:hi] for t in range(n_t)]
        cnt = float(min(PAST_LEN + 1, w))
        wp = wp_ref[g].astype(BF16)
        for t in range(n_t):
            s = ext[POOL_BUF + t]
            for k in range(1, w):
                s = s + ext[POOL_BUF + t - k]
            pooled = s / cnt - ext[POOL_BUF + t]
            mixed = _dot(pooled.astype(BF16), wp) * ps_ref[:, lo:hi]
            mixed_ref[t * n_seq:(t + 1) * n_seq, lo:hi] = mixed.astype(BF16)
        for k in range(POOL_BUF):
            st_ref[k * n_seq:(k + 1) * n_seq, lo:hi] = ext[n_t + k]


def _pool_path_prompt(x2d, g, w_in, w_pool, pool_scale, *, n_batch, seq, d_conv, d_pool, tm):
    m, d = x2d.shape
    n_groups = len(POOL_WINDOWS)
    d_group = d_pool // n_groups
    v_off = 3 * d_conv
    kern = functools.partial(_pool_prompt_kernel, tm=tm, blocks_per_seq=seq // tm, d_group=d_group)
    return pl.pallas_call(
        kern,
        grid=(m // tm,),
        in_specs=[
            pl.BlockSpec((tm, d), lambda i: (i, 0)),
            pl.BlockSpec((1, d), lambda i: (0, 0)),
            pl.BlockSpec((d, d_pool), lambda i: (0, v_off // d_pool)),
            pl.BlockSpec((n_groups, d_group, d_group), lambda i: (0, 0, 0)),
            pl.BlockSpec((1, d_pool), lambda i: (0, 0)),
        ],
        out_specs=[
            pl.BlockSpec((tm, d), lambda i: (i, 0)),
            pl.BlockSpec((tm, d_pool), lambda i: (i, 0)),
            pl.BlockSpec((1, POOL_CARRY_ROWS, d_pool), lambda i: ((i * tm) // seq, 0, 0)),
        ],
        out_shape=[
            jax.ShapeDtypeStruct((m, d), BF16),
            jax.ShapeDtypeStruct((m, d_pool), BF16),
            jax.ShapeDtypeStruct((n_batch, POOL_CARRY_ROWS, d_pool), F32),
        ],
        scratch_shapes=[pltpu.VMEM((POOL_CARRY_ROWS, d_pool), F32), pltpu.VMEM((d, d_pool), BF16)],
        compiler_params=_params(1),
        name="pool_path_prompt",
    )(x2d, g, w_in, w_pool, pool_scale)


def _pool_path_sample(xn, w_tail_bf, w_pool, pool_scale, state_tm, *, n_seq, n_t, d_pool):
    m, d = xn.shape
    n_groups = len(POOL_WINDOWS)
    d_group = d_pool // n_groups
    kern = functools.partial(_pool_sample_kernel, n_seq=n_seq, n_t=n_t, d_group=d_group)
    return pl.pallas_call(
        kern,
        grid=(1,),
        in_specs=[
            pl.BlockSpec((m, d), lambda i: (0, 0)),
            pl.BlockSpec((d, d_pool), lambda i: (0, 0)),
            pl.BlockSpec((n_groups, d_group, d_group), lambda i: (0, 0, 0)),
            pl.BlockSpec((1, d_pool), lambda i: (0, 0)),
            pl.BlockSpec((POOL_BUF * n_seq, d_pool), lambda i: (0, 0)),
        ],
        out_specs=[
            pl.BlockSpec((m, d_pool), lambda i: (0, 0)),
            pl.BlockSpec((POOL_BUF * n_seq, d_pool), lambda i: (0, 0)),
        ],
        out_shape=[
            jax.ShapeDtypeStruct((m, d_pool), BF16),
            jax.ShapeDtypeStruct((POOL_BUF * n_seq, d_pool), F32),
        ],
        compiler_params=_params(1),
        name="pool_path_sample",
    )(xn, w_tail_bf, w_pool, pool_scale, state_tm)


def _gate_merge_kernel(xn_ref, yc_ref, mx_ref, wgc_ref, wgp_ref, wbc_ref, wbp_ref, *rest, n_riders):
    rider_src = rest[:n_riders]
    o_ref = rest[n_riders]
    rider_dst = rest[n_riders + 1:2 * n_riders + 1]
    xn = xn_ref[...]
    gc = _dot(xn, wgc_ref[...])
    gp = _dot(xn, wgp_ref[...])
    y_conv = _dot(yc_ref[...], wbc_ref[...])
    y_pool = _dot(mx_ref[...], wbp_ref[...])
    o_ref[...] = (jax.nn.sigmoid(gc) * y_conv + jax.nn.sigmoid(gp) * y_pool).astype(BF16)
    _cast_riders(rider_src, rider_dst)


def _gate_merge(xn, ycin, mixed, w_tail_bf, w_brc_bf, w_brp_bf, slab_weights, *, gc_off, gp_off, tm, tn):
    m, d = xn.shape
    d_conv = ycin.shape[1]
    d_pool = mixed.shape[1]
    nct = d // tn
    riders = [_slab_rider(w, (m // tm) * nct, lambda i, j: i * nct + j) for w in slab_weights]
    out = pl.pallas_call(
        functools.partial(_gate_merge_kernel, n_riders=len(riders)),
        grid=(m // tm, nct),
        in_specs=[
            pl.BlockSpec((tm, d), lambda i, j: (i, 0)),
            pl.BlockSpec((tm, d_conv), lambda i, j: (i, 0)),
            pl.BlockSpec((tm, d_pool), lambda i, j: (i, 0)),
            pl.BlockSpec((d, tn), lambda i, j: (0, gc_off // tn + j)),
            pl.BlockSpec((d, tn), lambda i, j: (0, gp_off // tn + j)),
            pl.BlockSpec((d_conv, tn), lambda i, j: (0, j)),
            pl.BlockSpec((d_pool, tn), lambda i, j: (0, j)),
        ] + [r[0] for r in riders],
        out_specs=[pl.BlockSpec((tm, tn), lambda i, j: (i, j))] + [r[1] for r in riders],
        out_shape=[jax.ShapeDtypeStruct((m, d), BF16)] + [r[2] for r in riders],
        compiler_params=_params(2),
        name="gate_merge",
    )(xn, ycin, mixed, w_tail_bf, w_tail_bf, w_brc_bf, w_brp_bf, *slab_weights)
    return out[0], out[1:]


def _out_proj_kernel(x_ref, mg_ref, wo_ref, x1_ref):
    x1_ref[...] = x_ref[...] + _dot(mg_ref[...], wo_ref[...])


def _out_proj(x2d, merged, w_o_bf, *, tm, tn):
    m, d = x2d.shape
    return pl.pallas_call(
        _out_proj_kernel,
        grid=(d // tn, m // tm),
        in_specs=[
            pl.BlockSpec((tm, tn), lambda j, i: (i, j)),
            pl.BlockSpec((tm, d), lambda j, i: (i, 0)),
            pl.BlockSpec((d, tn), lambda j, i: (0, j)),
        ],
        out_specs=pl.BlockSpec((tm, tn), lambda j, i: (i, j)),
        out_shape=jax.ShapeDtypeStruct((m, d), F32),
        compiler_params=_params(2),
        name="out_proj_residual",
    )(x2d, merged, w_o_bf)


def _ffn_kernel(x1_ref, wg_ref, wu_ref, wd_ref, gf_ref, gl_ref, o_ref, hn_ref, *, final_norm):
    f = pl.program_id(1)

    @pl.when(f == 0)
    def _():
        x1 = x1_ref[...]
        hn_ref[...] = _rmsnorm(x1, gf_ref[...]).astype(BF16)
        o_ref[...] = x1

    hn = hn_ref[...]
    gate = _dot(hn, wg_ref[...])
    up = _dot(hn, wu_ref[...])
    ff = (gate * jax.nn.sigmoid(gate)) * up
    o_ref[...] += _dot(ff.astype(BF16), wd_ref[...])

    if final_norm:
        @pl.when(f == pl.num_programs(1) - 1)
        def _():
            o_ref[...] = _rmsnorm(o_ref[...], gl_ref[...])


def _ffn(x1, w_gate_bf, w_up_bf, w_down_bf, g_ffn, g_final, *, final_norm, tm, tf):
    m, d = x1.shape
    d_ff = w_gate_bf.shape[1]
    kern = functools.partial(_ffn_kernel, final_norm=final_norm)
    return pl.pallas_call(
        kern,
        grid=(m // tm, d_ff // tf),
        in_specs=[
            pl.BlockSpec((tm, d), lambda i, f: (i, 0)),
            pl.BlockSpec((d, tf), lambda i, f: (0, f)),
            pl.BlockSpec((d, tf), lambda i, f: (0, f)),
            pl.BlockSpec((tf, d), lambda i, f: (f, 0)),
            pl.BlockSpec((1, d), lambda i, f: (0, 0)),
            pl.BlockSpec((1, d), lambda i, f: (0, 0)),
        ],
        out_specs=pl.BlockSpec((tm, d), lambda i, f: (i, 0)),
        out_shape=jax.ShapeDtypeStruct((m, d), F32),
        scratch_shapes=[pltpu.VMEM((tm, d), BF16)],
        compiler_params=_params(2),
        name="swiglu_ffn",
    )(x1, w_gate_bf, w_up_bf, w_down_bf, g_ffn, g_final)


def _tiles(d_conv, d_ff):
    tm = 1024
    tm_pool = 512
    tn_conv = 512 if d_conv % 512 == 0 else 256
    tn_merge = 512
    tn_out = 1024
    tf = 512 if d_ff % 512 == 0 else 256
    return tm, tm_pool, tn_conv, tn_merge, tn_out, tf


def kernel(x_prompt, x_sample, state_conv, state_pool, norm_mix, w_in, conv_w, w_pool, pool_scale, w_br_conv,
           w_br_pool, w_o, norm_ffn, w_gate, w_up, w_down, norm_final):
    n_batch, seq, d = x_prompt.shape
    n_seq, n_t, _ = x_sample.shape
    depth = w_in.shape[0]
    d_conv = conv_w.shape[2]
    d_pool = pool_scale.shape[1]
    d_ff = w_gate.shape[2]
    gc_off = d_pool
    gp_off = gc_off + d
    tm, tm_pool, tn_conv, tn_merge, tn_out, tf = _tiles(d_conv, d_ff)
    assert seq % tm == 0 and seq % tm_pool == 0 and tm_pool >= POOL_CARRY_ROWS and n_t >= CONV_WIDTH - 1
    assert n_seq % 16 == 0 and gc_off % tn_merge == 0 and gp_off % tn_merge == 0

    yp = x_prompt.reshape(n_batch * seq, d)
    ys = jnp.transpose(x_sample, (1, 0, 2)).reshape(n_t * n_seq, d)
    g_final = norm_final.reshape(1, d)
    conv_p, pool_p, conv_s, pool_s = [], [], [], []
    for l in range(depth):
        g_mix = norm_mix[l].reshape(1, d)
        g_ffn = norm_ffn[l].reshape(1, d)
        ps = pool_scale[l].reshape(1, d_pool)
        last = l == depth - 1

        xn, mixed, st_p = _pool_path_prompt(yp, g_mix, w_in[l], w_pool[l], ps, n_batch=n_batch, seq=seq,
                                            d_conv=d_conv, d_pool=d_pool, tm=tm_pool)
        ycin, st_c, w_tail_bf, w_brc_bf, w_brp_bf = _conv_path_prompt(
            xn, conv_w[l], w_in[l], (w_br_conv[l], w_br_pool[l]), seq=seq, d_conv=d_conv, tm=tm, tn=tn_conv)
        merged, (w_o_bf, w_down_bf, w_gate_bf, w_up_bf) = _gate_merge(
            xn, ycin, mixed, w_tail_bf, w_brc_bf, w_brp_bf, (w_o[l], w_down[l], w_gate[l], w_up[l]),
            gc_off=gc_off, gp_off=gp_off, tm=tm, tn=tn_merge)
        x1 = _out_proj(yp, merged, w_o_bf, tm=tm, tn=tn_out)
        yp = _ffn(x1, w_gate_bf, w_up_bf, w_down_bf, g_ffn, g_final, final_norm=last, tm=tm, tf=tf)
        bps = seq // tm
        conv_p.append(st_c[bps - 1::bps, SUBLANES - (CONV_WIDTH - 1):, :])
        pool_p.append(st_p[:, POOL_CARRY_ROWS - POOL_BUF:, :])

        sc_tm = jnp.transpose(state_conv[l], (1, 0, 2)).reshape((CONV_WIDTH - 1) * n_seq, d_conv)
        sp_tm = jnp.transpose(state_pool[l], (1, 0, 2)).reshape(POOL_BUF * n_seq, d_pool)
        xn, ycin, st_c = _conv_path_sample(ys, g_mix, w_in[l], conv_w[l], sc_tm, n_seq=n_seq, n_t=n_t,
                                           d_conv=d_conv, tn=tn_conv)
        mixed, st_p = _pool_path_sample(xn, w_tail_bf, w_pool[l], ps, sp_tm, n_seq=n_seq, n_t=n_t, d_pool=d_pool)
        merged, _ = _gate_merge(xn, ycin, mixed, w_tail_bf, w_brc_bf, w_brp_bf, (),
                                gc_off=gc_off, gp_off=gp_off, tm=n_t * n_seq, tn=tn_merge)
        x1 = _out_proj(ys, merged, w_o_bf, tm=n_t * n_seq, tn=tn_out)
        ys = _ffn(x1, w_gate_bf, w_up_bf, w_down_bf, g_ffn, g_final, final_norm=last, tm=n_t * n_seq, tf=tf)
        conv_s.append(jnp.transpose(st_c.reshape(CONV_WIDTH - 1, n_seq, d_conv), (1, 0, 2)))
        pool_s.append(jnp.transpose(st_p.reshape(POOL_BUF, n_seq, d_pool), (1, 0, 2)))

    y_prompt = yp.reshape(n_batch, seq, d)
    y_sample = jnp.transpose(ys.reshape(n_t, n_seq, d), (1, 0, 2))
    return (y_prompt, y_sample, jnp.stack(conv_p, axis=0), jnp.stack(pool_p, axis=0),
            jnp.stack(conv_s, axis=0), jnp.stack(pool_s, axis=0))
```

```python
import functools

import jax
import jax.numpy as jnp
from jax import lax
from jax.experimental import pallas as pl
from jax.experimental.pallas import tpu as pltpu

EPS = 1e-6
CONV_WIDTH = 3
POOL_WINDOWS = (2, 4, 8, 16)
POOL_BUF = max(POOL_WINDOWS) - 1
PAST_LEN = 16384

SUBLANES = 8
HEAD_ROWS = 16
POOL_CARRY_ROWS = 16
VMEM_LIMIT_BYTES = 56 * 1024 * 1024

BF16 = jnp.bfloat16
F32 = jnp.float32


def _dot(a, b):
    return jnp.dot(a, b, preferred_element_type=F32)


def _rmsnorm(x, g):
    r = lax.rsqrt(jnp.mean(x * x, axis=-1, keepdims=True) + EPS)
    return (x * r) * g


def _params(n_axes):
    return pltpu.CompilerParams(dimension_semantics=("arbitrary",) * n_axes,
                                vmem_limit_bytes=VMEM_LIMIT_BYTES)


def _cast_riders(src_refs, dst_refs):
    for src, dst in zip(src_refs, dst_refs):
        dst[...] = src[...].astype(BF16)


def _slab_rider(w, n_steps, step_of):
    rows, cols = w.shape
    slab = rows // n_steps
    assert slab * n_steps == rows and slab % HEAD_ROWS == 0
    spec = pl.BlockSpec((slab, cols), lambda *ids: (step_of(*ids), 0))
    return spec, spec, jax.ShapeDtypeStruct((rows, cols), BF16)


def _col_rider(w, col0, col1, n_steps, step_of, *, width):
    rows = w.shape[0]
    n_col = (col1 - col0) // width
    assert n_col * width == col1 - col0 and col0 % width == 0 and n_col <= n_steps
    n_row = 1
    while n_row * 2 * n_col <= n_steps and rows % (n_row * 2) == 0:
        n_row *= 2
    rb = rows // n_row
    assert rb % HEAD_ROWS == 0
    n_blk = n_row * n_col

    def blk(*ids):
        return jnp.minimum(step_of(*ids), n_blk - 1)

    in_spec = pl.BlockSpec((rb, width), lambda *ids: (blk(*ids) // n_col, col0 // width + blk(*ids) % n_col))
    out_spec = pl.BlockSpec((rb, width), lambda *ids: (blk(*ids) // n_col, blk(*ids) % n_col))
    return in_spec, out_spec, jax.ShapeDtypeStruct((rows, col1 - col0), BF16)


def _conv_prompt_kernel(xn_ref, wh_ref, wb_ref, wc_ref, cw_ref, *rest, tm, blocks_per_seq, n_riders):
    rider_src = rest[:n_riders]
    y_ref, st_ref = rest[n_riders:n_riders + 2]
    rider_dst = rest[n_riders + 2:2 * n_riders + 2]
    carry_ref, whb_ref, wbb_ref, wcb_ref = rest[2 * n_riders + 2:]
    i = pl.program_id(1)

    @pl.when(i == 0)
    def _():
        whb_ref[...] = wh_ref[...].astype(BF16)
        wbb_ref[...] = wb_ref[...].astype(BF16)
        wcb_ref[...] = wc_ref[...].astype(BF16)

    @pl.when(i % blocks_per_seq == 0)
    def _():
        carry_ref[...] = jnp.zeros(carry_ref.shape, F32)

    xn = xn_ref[...]
    c = _dot(xn, wcb_ref[...])
    h = _dot(xn, whb_ref[...])
    u = c * h
    b = _dot(xn, wbb_ref[...])
    w0 = cw_ref[0:1, :]
    w1 = cw_ref[1:2, :]
    w2 = cw_ref[2:3, :]
    conv = w0 * pltpu.roll(u, 2, 0) + w1 * pltpu.roll(u, 1, 0) + w2 * u
    y_ref[...] = (b * conv).astype(BF16)
    head = jnp.concatenate([carry_ref[...], u[0:HEAD_ROWS]], axis=0)
    conv_head = w0 * pltpu.roll(head, 2, 0) + w1 * pltpu.roll(head, 1, 0) + w2 * head
    y_ref[0:HEAD_ROWS, :] = (b[0:HEAD_ROWS] * conv_head[SUBLANES:]).astype(BF16)
    tail = u[tm - SUBLANES:tm]
    carry_ref[...] = tail
    st_ref[0] = tail
    _cast_riders(rider_src, rider_dst)


def _conv_sample_kernel(x_ref, g_ref, wh_ref, wb_ref, wc_ref, cw_ref, s_ref, xn_ref, y_ref, st_ref,
                        *, n_seq, n_t):
    j = pl.program_id(0)

    @pl.when(j == 0)
    def _():
        xn_ref[...] = _rmsnorm(x_ref[...], g_ref[...]).astype(BF16)

    xn = xn_ref[...]
    c = _dot(xn, wc_ref[...].astype(BF16))
    h = _dot(xn, wh_ref[...].astype(BF16))
    u = c * h
    b = _dot(xn, wb_ref[...].astype(BF16))
    w0 = cw_ref[0:1, :]
    w1 = cw_ref[1:2, :]
    w2 = cw_ref[2:3, :]
    n_state = CONV_WIDTH - 1
    ext = [s_ref[k * n_seq:(k + 1) * n_seq, :] for k in range(n_state)]
    ext += [u[t * n_seq:(t + 1) * n_seq] for t in range(n_t)]
    for t in range(n_t):
        conv = w0 * ext[t] + w1 * ext[t + 1] + w2 * ext[t + 2]
        y_ref[t * n_seq:(t + 1) * n_seq, :] = (b[t * n_seq:(t + 1) * n_seq] * conv).astype(BF16)
    for k in range(n_state):
        st_ref[k * n_seq:(k + 1) * n_seq, :] = ext[n_t + k]


def _conv_path_prompt(xn, conv_w, w_in, slab_weights, *, seq, d_conv, tm, tn):
    m, d = xn.shape
    nct = d_conv // tn
    n_i = m // tm
    n_steps = nct * n_i
    step_of = lambda j, i: j * n_i + i
    riders = [_col_rider(w_in, 3 * d_conv, w_in.shape[1], n_steps, step_of, width=tn * 2)]
    riders += [_slab_rider(w, n_steps, step_of) for w in slab_weights]
    kern = functools.partial(_conv_prompt_kernel, tm=tm, blocks_per_seq=seq // tm, n_riders=len(riders))
    return pl.pallas_call(
        kern,
        grid=(nct, n_i),
        in_specs=[
            pl.BlockSpec((tm, d), lambda j, i: (i, 0)),
            pl.BlockSpec((d, tn), lambda j, i: (0, j)),
            pl.BlockSpec((d, tn), lambda j, i: (0, nct + j)),
            pl.BlockSpec((d, tn), lambda j, i: (0, 2 * nct + j)),
            pl.BlockSpec((CONV_WIDTH, tn), lambda j, i: (0, j)),
        ] + [r[0] for r in riders],
        out_specs=[
            pl.BlockSpec((tm, tn), lambda j, i: (i, j)),
            pl.BlockSpec((1, SUBLANES, tn), lambda j, i: (i, 0, j)),
        ] + [r[1] for r in riders],
        out_shape=[
            jax.ShapeDtypeStruct((m, d_conv), BF16),
            jax.ShapeDtypeStruct((n_i, SUBLANES, d_conv), F32),
        ] + [r[2] for r in riders],
        scratch_shapes=[pltpu.VMEM((SUBLANES, tn), F32)] + [pltpu.VMEM((d, tn), BF16)] * 3,
        compiler_params=_params(2),
        name="conv_path_prompt",
    )(xn, w_in, w_in, w_in, conv_w, w_in, *slab_weights)


def _conv_path_sample(x2d, g, w_in, conv_w, state_tm, *, n_seq, n_t, d_conv, tn):
    m, d = x2d.shape
    nct = d_conv // tn
    n_state = CONV_WIDTH - 1
    kern = functools.partial(_conv_sample_kernel, n_seq=n_seq, n_t=n_t)
    return pl.pallas_call(
        kern,
        grid=(nct,),
        in_specs=[
            pl.BlockSpec((m, d), lambda j: (0, 0)),
            pl.BlockSpec((1, d), lambda j: (0, 0)),
            pl.BlockSpec((d, tn), lambda j: (0, j)),
            pl.BlockSpec((d, tn), lambda j: (0, nct + j)),
            pl.BlockSpec((d, tn), lambda j: (0, 2 * nct + j)),
            pl.BlockSpec((CONV_WIDTH, tn), lambda j: (0, j)),
            pl.BlockSpec((n_state * n_seq, tn), lambda j: (0, j)),
        ],
        out_specs=[
            pl.BlockSpec((m, d), lambda j: (0, 0)),
            pl.BlockSpec((m, tn), lambda j: (0, j)),
            pl.BlockSpec((n_state * n_seq, tn), lambda j: (0, j)),
        ],
        out_shape=[
            jax.ShapeDtypeStruct((m, d), BF16),
            jax.ShapeDtypeStruct((m, d_conv), BF16),
            jax.ShapeDtypeStruct((n_state * n_seq, d_conv), F32),
        ],
        compiler_params=_params(1),
        name="conv_path_sample",
    )(x2d, g, w_in, w_in, w_in, conv_w, state_tm)


def _pool_prompt_kernel(x_ref, g_ref, wv_ref, wp_ref, ps_ref, xn_ref, mixed_ref, st_ref, carry_ref, wvb_ref,
                        *, tm, blocks_per_seq, d_group):
    i = pl.program_id(0)
    blk = i % blocks_per_seq

    @pl.when(i == 0)
    def _():
        wvb_ref[...] = wv_ref[...].astype(BF16)

    @pl.when(blk == 0)
    def _():
        carry_ref[...] = jnp.zeros(carry_ref.shape, F32)

    xn = _rmsnorm(x_ref[...], g_ref[...]).astype(BF16)
    xn_ref[...] = xn
    pos = blk * tm + lax.broadcasted_iota(jnp.int32, (tm, 1), 0)
    n_groups = len(POOL_WINDOWS)
    for g in sorted(range(n_groups), key=lambda q: (-(q // 2), q)):
        w = POOL_WINDOWS[g]
        lo, hi = g * d_group, (g + 1) * d_group
        if g % 2 == 0:
            v_pair = _dot(xn, wvb_ref[:, lo:lo + 2 * d_group])
        v = v_pair[:, (g % 2) * d_group:(g % 2 + 1) * d_group]
        s = jnp.concatenate([carry_ref[:, lo:hi], v], axis=0)
        tail = v[tm - POOL_CARRY_ROWS:tm]
        carry_ref[:, lo:hi] = tail
        st_ref[0, :, lo:hi] = tail
        k = 1
        while k < w:
            s = s + pltpu.roll(s, k, 0)
            k *= 2
        cnt = jnp.minimum(pos + 1, w).astype(F32)
        pooled = s[POOL_CARRY_ROWS:] / cnt - v
        mixed = _dot(pooled.astype(BF16), wp_ref[g].astype(BF16)) * ps_ref[:, lo:hi]
        mixed_ref[:, lo:hi] = mixed.astype(BF16)


def _pool_sample_kernel(xn_ref, wv_ref, wp_ref, ps_ref, s_ref, mixed_ref, st_ref, *, n_seq, n_t, d_group):
    v = _dot(xn_ref[...], wv_ref[...])
    for g, w in enumerate(POOL_WINDOWS):
        lo, hi = g * d_group, (g + 1) * d_group
        ext = [s_ref[k * n_seq:(k + 1) * n_seq, lo:hi] for k in range(POOL_BUF)]
        ext += [v[t * n_seq:(t + 1) * n_seq, lo:hi] for t in range(n_t)]
        cnt = float(min(PAST_LEN + 1, w))
        wp = wp_ref[g].astype(BF16)
        for t in range(n_t):
            s = ext[POOL_BUF + t]
            for k in range(1, w):
                s = s + ext[POOL_BUF + t - k]
            pooled = s / cnt - ext[POOL_BUF + t]
            mixed = _dot(pooled.astype(BF16), wp) * ps_ref[:, lo:hi]
            mixed_ref[t * n_seq:(t + 1) * n_seq, lo:hi] = mixed.astype(BF16)
        for k in range(POOL_BUF):
            st_ref[k * n_seq:(k + 1) * n_seq, lo:hi] = ext[n_t + k]


def _pool_path_prompt(x2d, g, w_in, w_pool, pool_scale, *, n_batch, seq, d_conv, d_pool, tm):
    m, d = x2d.shape
    n_groups = len(POOL_WINDOWS)
    d_group = d_pool // n_groups
    v_off = 3 * d_conv
    kern = functools.partial(_pool_prompt_kernel, tm=tm, blocks_per_seq=seq // tm, d_group=d_group)
    return pl.pallas_call(
        kern,
        grid=(m // tm,),
        in_specs=[
            pl.BlockSpec((tm, d), lambda i: (i, 0)),
            pl.BlockSpec((1, d), lambda i: (0, 0)),
            pl.BlockSpec((d, d_pool), lambda i: (0, v_off // d_pool)),
            pl.BlockSpec((n_groups, d_group, d_group), lambda i: (0, 0, 0)),
            pl.BlockSpec((1, d_pool), lambda i: (0, 0)),
        ],
        out_specs=[
            pl.BlockSpec((tm, d), lambda i: (i, 0)),
            pl.BlockSpec((tm, d_pool), lambda i: (i, 0)),
            pl.BlockSpec((1, POOL_CARRY_ROWS, d_pool), lambda i: ((i * tm) // seq, 0, 0)),
        ],
        out_shape=[
            jax.ShapeDtypeStruct((m, d), BF16),
            jax.ShapeDtypeStruct((m, d_pool), BF16),
            jax.ShapeDtypeStruct((n_batch, POOL_CARRY_ROWS, d_pool), F32),
        ],
        scratch_shapes=[pltpu.VMEM((POOL_CARRY_ROWS, d_pool), F32), pltpu.VMEM((d, d_pool), BF16)],
        compiler_params=_params(1),
        name="pool_path_prompt",
    )(x2d, g, w_in, w_pool, pool_scale)


def _pool_path_sample(xn, w_tail_bf, w_pool, pool_scale, state_tm, *, n_seq, n_t, d_pool):
    m, d = xn.shape
    n_groups = len(POOL_WINDOWS)
    d_group = d_pool // n_groups
    kern = functools.partial(_pool_sample_kernel, n_seq=n_seq, n_t=n_t, d_group=d_group)
    return pl.pallas_call(
        kern,
        grid=(1,),
        in_specs=[
            pl.BlockSpec((m, d), lambda i: (0, 0)),
            pl.BlockSpec((d, d_pool), lambda i: (0, 0)),
            pl.BlockSpec((n_groups, d_group, d_group), lambda i: (0, 0, 0)),
            pl.BlockSpec((1, d_pool), lambda i: (0, 0)),
            pl.BlockSpec((POOL_BUF * n_seq, d_pool), lambda i: (0, 0)),
        ],
        out_specs=[
            pl.BlockSpec((m, d_pool), lambda i: (0, 0)),
            pl.BlockSpec((POOL_BUF * n_seq, d_pool), lambda i: (0, 0)),
        ],
        out_shape=[
            jax.ShapeDtypeStruct((m, d_pool), BF16),
            jax.ShapeDtypeStruct((POOL_BUF * n_seq, d_pool), F32),
        ],
        compiler_params=_params(1),
        name="pool_path_sample",
    )(xn, w_tail_bf, w_pool, pool_scale, state_tm)


def _gate_merge_kernel(xn_ref, yc_ref, mx_ref, wgc_ref, wgp_ref, wbc_ref, wbp_ref, *rest, n_riders):
    rider_src = rest[:n_riders]
    o_ref = rest[n_riders]
    rider_dst = rest[n_riders + 1:2 * n_riders + 1]
    xn = xn_ref[...]
    gc = _dot(xn, wgc_ref[...])
    gp = _dot(xn, wgp_ref[...])
    y_conv = _dot(yc_ref[...], wbc_ref[...])
    y_pool = _dot(mx_ref[...], wbp_ref[...])
    o_ref[...] = (jax.nn.sigmoid(gc) * y_conv + jax.nn.sigmoid(gp) * y_pool).astype(BF16)
    _cast_riders(rider_src, rider_dst)


def _gate_merge(xn, ycin, mixed, w_tail_bf, w_brc_bf, w_brp_bf, slab_weights, *, gc_off, gp_off, tm, tn):
    m, d = xn.shape
    d_conv = ycin.shape[1]
    d_pool = mixed.shape[1]
    nct = d // tn
    riders = [_slab_rider(w, (m // tm) * nct, lambda i, j: i * nct + j) for w in slab_weights]
    out = pl.pallas_call(
        functools.partial(_gate_merge_kernel, n_riders=len(riders)),
        grid=(m // tm, nct),
        in_specs=[
            pl.BlockSpec((tm, d), lambda i, j: (i, 0)),
            pl.BlockSpec((tm, d_conv), lambda i, j: (i, 0)),
            pl.BlockSpec((tm, d_pool), lambda i, j: (i, 0)),
            pl.BlockSpec((d, tn), lambda i, j: (0, gc_off // tn + j)),
            pl.BlockSpec((d, tn), lambda i, j: (0, gp_off // tn + j)),
            pl.BlockSpec((d_conv, tn), lambda i, j: (0, j)),
            pl.BlockSpec((d_pool, tn), lambda i, j: (0, j)),
        ] + [r[0] for r in riders],
        out_specs=[pl.BlockSpec((tm, tn), lambda i, j: (i, j))] + [r[1] for r in riders],
        out_shape=[jax.ShapeDtypeStruct((m, d), BF16)] + [r[2] for r in riders],
        compiler_params=_params(2),
        name="gate_merge",
    )(xn, ycin, mixed, w_tail_bf, w_tail_bf, w_brc_bf, w_brp_bf, *slab_weights)
    return out[0], out[1:]


def _out_proj_kernel(x_ref, mg_ref, wo_ref, x1_ref):
    x1_ref[...] = x_ref[...] + _dot(mg_ref[...], wo_ref[...])


def _out_proj(x2d, merged, w_o_bf, *, tm, tn):
    m, d = x2d.shape
    return pl.pallas_call(
        _out_proj_kernel,
        grid=(d // tn, m // tm),
        in_specs=[
            pl.BlockSpec((tm, tn), lambda j, i: (i, j)),
            pl.BlockSpec((tm, d), lambda j, i: (i, 0)),
            pl.BlockSpec((d, tn), lambda j, i: (0, j)),
        ],
        out_specs=pl.BlockSpec((tm, tn), lambda j, i: (i, j)),
        out_shape=jax.ShapeDtypeStruct((m, d), F32),
        compiler_params=_params(2),
        name="out_proj_residual",
    )(x2d, merged, w_o_bf)


def _ffn_kernel(x1_ref, wg_ref, wu_ref, wd_ref, gf_ref, gl_ref, o_ref, hn_ref, *, final_norm):
    f = pl.program_id(1)

    @pl.when(f == 0)
    def _():
        x1 = x1_ref[...]
        hn_ref[...] = _rmsnorm(x1, gf_ref[...]).astype(BF16)
        o_ref[...] = x1

    hn = hn_ref[...]
    half = wg_ref.shape[1] // 2
    ff = []
    for k in range(2):
        cols = slice(k * half, (k + 1) * half)
        gate = _dot(hn, wg_ref[:, cols])
        up = _dot(hn, wu_ref[:, cols])
        ff.append(((gate * jax.nn.sigmoid(gate)) * up).astype(BF16))
    o_ref[...] += _dot(jnp.concatenate(ff, axis=1), wd_ref[...])

    if final_norm:
        @pl.when(f == pl.num_programs(1) - 1)
        def _():
            o_ref[...] = _rmsnorm(o_ref[...], gl_ref[...])


def _ffn(x1, w_gate_bf, w_up_bf, w_down_bf, g_ffn, g_final, *, final_norm, tm, tf):
    m, d = x1.shape
    d_ff = w_gate_bf.shape[1]
    kern = functools.partial(_ffn_kernel, final_norm=final_norm)
    return pl.pallas_call(
        kern,
        grid=(m // tm, d_ff // tf),
        in_specs=[
            pl.BlockSpec((tm, d), lambda i, f: (i, 0)),
            pl.BlockSpec((d, tf), lambda i, f: (0, f)),
            pl.BlockSpec((d, tf), lambda i, f: (0, f)),
            pl.BlockSpec((tf, d), lambda i, f: (f, 0)),
            pl.BlockSpec((1, d), lambda i, f: (0, 0)),
            pl.BlockSpec((1, d), lambda i, f: (0, 0)),
        ],
        out_specs=pl.BlockSpec((tm, d), lambda i, f: (i, 0)),
        out_shape=jax.ShapeDtypeStruct((m, d), F32),
        scratch_shapes=[pltpu.VMEM((tm, d), BF16)],
        compiler_params=_params(2),
        name="swiglu_ffn",
    )(x1, w_gate_bf, w_up_bf, w_down_bf, g_ffn, g_final)


def _tiles(d_conv, d_ff):
    tm = 1024
    tm_pool = 512
    tn_conv = 512 if d_conv % 512 == 0 else 256
    tn_merge = 512
    tm_out = 512
    tn_out = 2048
    tf = 512 if d_ff % 512 == 0 else 256
    return tm, tm_pool, tm_out, tn_conv, tn_merge, tn_out, tf


def kernel(x_prompt, x_sample, state_conv, state_pool, norm_mix, w_in, conv_w, w_pool, pool_scale, w_br_conv,
           w_br_pool, w_o, norm_ffn, w_gate, w_up, w_down, norm_final):
    n_batch, seq, d = x_prompt.shape
    n_seq, n_t, _ = x_sample.shape
    depth = w_in.shape[0]
    d_conv = conv_w.shape[2]
    d_pool = pool_scale.shape[1]
    d_ff = w_gate.shape[2]
    gc_off = d_pool
    gp_off = gc_off + d
    tm, tm_pool, tm_out, tn_conv, tn_merge, tn_out, tf = _tiles(d_conv, d_ff)
    assert seq % tm == 0 and seq % tm_pool == 0 and tm_pool >= POOL_CARRY_ROWS and n_t >= CONV_WIDTH - 1
    assert (n_batch * seq) % tm_out == 0 and (n_t * n_seq) % tm_out == 0
    assert n_seq % 16 == 0 and gc_off % tn_merge == 0 and gp_off % tn_merge == 0

    yp = x_prompt.reshape(n_batch * seq, d)
    ys = jnp.transpose(x_sample, (1, 0, 2)).reshape(n_t * n_seq, d)
    g_final = norm_final.reshape(1, d)
    conv_p, pool_p, conv_s, pool_s = [], [], [], []
    for l in range(depth):
        g_mix = norm_mix[l].reshape(1, d)
        g_ffn = norm_ffn[l].reshape(1, d)
        ps = pool_scale[l].reshape(1, d_pool)
        last = l == depth - 1

        xn, mixed, st_p = _pool_path_prompt(yp, g_mix, w_in[l], w_pool[l], ps, n_batch=n_batch, seq=seq,
                                            d_conv=d_conv, d_pool=d_pool, tm=tm_pool)
        ycin, st_c, w_tail_bf, w_brc_bf, w_brp_bf = _conv_path_prompt(
            xn, conv_w[l], w_in[l], (w_br_conv[l], w_br_pool[l]), seq=seq, d_conv=d_conv, tm=tm, tn=tn_conv)
        merged, (w_o_bf, w_down_bf, w_gate_bf, w_up_bf) = _gate_merge(
            xn, ycin, mixed, w_tail_bf, w_brc_bf, w_brp_bf, (w_o[l], w_down[l], w_gate[l], w_up[l]),
            gc_off=gc_off, gp_off=gp_off, tm=tm, tn=tn_merge)
        x1 = _out_proj(yp, merged, w_o_bf, tm=tm_out, tn=tn_out)
        yp = _ffn(x1, w_gate_bf, w_up_bf, w_down_bf, g_ffn, g_final, final_norm=last, tm=tm, tf=tf)
        bps = seq // tm
        conv_p.append(st_c[bps - 1::bps, SUBLANES - (CONV_WIDTH - 1):, :])
        pool_p.append(st_p[:, POOL_CARRY_ROWS - POOL_BUF:, :])

        sc_tm = jnp.transpose(state_conv[l], (1, 0, 2)).reshape((CONV_WIDTH - 1) * n_seq, d_conv)
        sp_tm = jnp.transpose(state_pool[l], (1, 0, 2)).reshape(POOL_BUF * n_seq, d_pool)
        xn, ycin, st_c = _conv_path_sample(ys, g_mix, w_in[l], conv_w[l], sc_tm, n_seq=n_seq, n_t=n_t,
                                           d_conv=d_conv, tn=tn_conv)
        mixed, st_p = _pool_path_sample(xn, w_tail_bf, w_pool[l], ps, sp_tm, n_seq=n_seq, n_t=n_t, d_pool=d_pool)
        merged, _ = _gate_merge(xn, ycin, mixed, w_tail_bf, w_brc_bf, w_brp_bf, (),
                                gc_off=gc_off, gp_off=gp_off, tm=n_t * n_seq, tn=tn_merge)
        x1 = _out_proj(ys, merged, w_o_bf, tm=tm_out, tn=tn_out)
        ys = _ffn(x1, w_gate_bf, w_up_bf, w_down_bf, g_ffn, g_final, final_norm=last, tm=n_t * n_seq, tf=tf)
        conv_s.append(jnp.transpose(st_c.reshape(CONV_WIDTH - 1, n_seq, d_conv), (1, 0, 2)))
        pool_s.append(jnp.transpose(st_p.reshape(POOL_BUF, n_seq, d_pool), (1, 0, 2)))

    y_prompt = yp.reshape(n_batch, seq, d)
    y_sample = jnp.transpose(ys.reshape(n_t, n_seq, d), (1, 0, 2))
    return (y_prompt, y_sample, jnp.stack(conv_p, axis=0), jnp.stack(pool_p, axis=0),
            jnp.stack(conv_s, axis=0), jnp.stack(pool_s, axis=0))
```

```python
import functools

import jax
import jax.numpy as jnp
from jax import lax
from jax.experimental import pallas as pl
from jax.experimental.pallas import tpu as pltpu

EPS = 1e-6
CONV_WIDTH = 3
POOL_WINDOWS = (2, 4, 8, 16)
POOL_BUF = max(POOL_WINDOWS) - 1
PAST_LEN = 16384

SUBLANES = 8
HEAD_ROWS = 16
POOL_CARRY_ROWS = 16
VMEM_LIMIT_BYTES = 56 * 1024 * 1024

BF16 = jnp.bfloat16
F32 = jnp.float32


def _dot(a, b):
    return jnp.dot(a, b, preferred_element_type=F32)


def _rmsnorm(x, g):
    r = lax.rsqrt(jnp.mean(x * x, axis=-1, keepdims=True) + EPS)
    return (x * r) * g


def _params(n_axes):
    return pltpu.CompilerParams(dimension_semantics=("arbitrary",) * n_axes,
                                vmem_limit_bytes=VMEM_LIMIT_BYTES)


def _cast_riders(src_refs, dst_refs):
    for src, dst in zip(src_refs, dst_refs):
        dst[...] = src[...].astype(BF16)


def _slab_rider(w, n_steps, step_of):
    rows, cols = w.shape
    slab = rows // n_steps
    assert slab * n_steps == rows and slab % HEAD_ROWS == 0
    spec = pl.BlockSpec((slab, cols), lambda *ids: (step_of(*ids), 0))
    return spec, spec, jax.ShapeDtypeStruct((rows, cols), BF16)


def _col_rider(w, col0, col1, n_steps, step_of, *, width):
    rows = w.shape[0]
    n_col = (col1 - col0) // width
    assert n_col * width == col1 - col0 and col0 % width == 0 and n_col <= n_steps
    n_row = 1
    while n_row * 2 * n_col <= n_steps and rows % (n_row * 2) == 0:
        n_row *= 2
    rb = rows // n_row
    assert rb % HEAD_ROWS == 0
    n_blk = n_row * n_col

    def blk(*ids):
        return jnp.minimum(step_of(*ids), n_blk - 1)

    in_spec = pl.BlockSpec((rb, width), lambda *ids: (blk(*ids) // n_col, col0 // width + blk(*ids) % n_col))
    out_spec = pl.BlockSpec((rb, width), lambda *ids: (blk(*ids) // n_col, blk(*ids) % n_col))
    return in_spec, out_spec, jax.ShapeDtypeStruct((rows, col1 - col0), BF16)


def _conv_prompt_kernel(xn_ref, wh_ref, wb_ref, wc_ref, cw_ref, *rest, tm, blocks_per_seq, n_riders):
    rider_src = rest[:n_riders]
    y_ref, st_ref = rest[n_riders:n_riders + 2]
    rider_dst = rest[n_riders + 2:2 * n_riders + 2]
    carry_ref, whb_ref, wbb_ref, wcb_ref = rest[2 * n_riders + 2:]
    i = pl.program_id(1)

    @pl.when(i == 0)
    def _():
        whb_ref[...] = wh_ref[...].astype(BF16)
        wbb_ref[...] = wb_ref[...].astype(BF16)
        wcb_ref[...] = wc_ref[...].astype(BF16)

    @pl.when(i % blocks_per_seq == 0)
    def _():
        carry_ref[...] = jnp.zeros(carry_ref.shape, F32)

    xn = xn_ref[...]
    c = _dot(xn, wcb_ref[...])
    h = _dot(xn, whb_ref[...])
    u = c * h
    b = _dot(xn, wbb_ref[...])
    w0 = cw_ref[0:1, :]
    w1 = cw_ref[1:2, :]
    w2 = cw_ref[2:3, :]
    conv = w0 * pltpu.roll(u, 2, 0) + w1 * pltpu.roll(u, 1, 0) + w2 * u
    y_ref[...] = (b * conv).astype(BF16)
    head = jnp.concatenate([carry_ref[...], u[0:HEAD_ROWS]], axis=0)
    conv_head = w0 * pltpu.roll(head, 2, 0) + w1 * pltpu.roll(head, 1, 0) + w2 * head
    y_ref[0:HEAD_ROWS, :] = (b[0:HEAD_ROWS] * conv_head[SUBLANES:]).astype(BF16)
    tail = u[tm - SUBLANES:tm]
    carry_ref[...] = tail
    st_ref[0] = tail
    _cast_riders(rider_src, rider_dst)


def _conv_sample_kernel(x_ref, g_ref, wh_ref, wb_ref, wc_ref, cw_ref, s_ref, xn_ref, y_ref, st_ref,
                        *, n_seq, n_t):
    j = pl.program_id(0)

    @pl.when(j == 0)
    def _():
        xn_ref[...] = _rmsnorm(x_ref[...], g_ref[...]).astype(BF16)

    xn = xn_ref[...]
    c = _dot(xn, wc_ref[...].astype(BF16))
    h = _dot(xn, wh_ref[...].astype(BF16))
    u = c * h
    b = _dot(xn, wb_ref[...].astype(BF16))
    w0 = cw_ref[0:1, :]
    w1 = cw_ref[1:2, :]
    w2 = cw_ref[2:3, :]
    n_state = CONV_WIDTH - 1
    ext = [s_ref[k * n_seq:(k + 1) * n_seq, :] for k in range(n_state)]
    ext += [u[t * n_seq:(t + 1) * n_seq] for t in range(n_t)]
    for t in range(n_t):
        conv = w0 * ext[t] + w1 * ext[t + 1] + w2 * ext[t + 2]
        y_ref[t * n_seq:(t + 1) * n_seq, :] = (b[t * n_seq:(t + 1) * n_seq] * conv).astype(BF16)
    for k in range(n_state):
        st_ref[k * n_seq:(k + 1) * n_seq, :] = ext[n_t + k]


def _conv_path_prompt(xn, conv_w, w_in, slab_weights, *, seq, d_conv, tm, tn):
    m, d = xn.shape
    nct = d_conv // tn
    n_i = m // tm
    n_steps = nct * n_i
    step_of = lambda j, i: j * n_i + i
    riders = [_col_rider(w_in, 3 * d_conv, w_in.shape[1], n_steps, step_of, width=tn * 2)]
    riders += [_slab_rider(w, n_steps, step_of) for w in slab_weights]
    kern = functools.partial(_conv_prompt_kernel, tm=tm, blocks_per_seq=seq // tm, n_riders=len(riders))
    return pl.pallas_call(
        kern,
        grid=(nct, n_i),
        in_specs=[
            pl.BlockSpec((tm, d), lambda j, i: (i, 0)),
            pl.BlockSpec((d, tn), lambda j, i: (0, j)),
            pl.BlockSpec((d, tn), lambda j, i: (0, nct + j)),
            pl.BlockSpec((d, tn), lambda j, i: (0, 2 * nct + j)),
            pl.BlockSpec((CONV_WIDTH, tn), lambda j, i: (0, j)),
        ] + [r[0] for r in riders],
        out_specs=[
            pl.BlockSpec((tm, tn), lambda j, i: (i, j)),
            pl.BlockSpec((1, SUBLANES, tn), lambda j, i: (i, 0, j)),
        ] + [r[1] for r in riders],
        out_shape=[
            jax.ShapeDtypeStruct((m, d_conv), BF16),
            jax.ShapeDtypeStruct((n_i, SUBLANES, d_conv), F32),
        ] + [r[2] for r in riders],
        scratch_shapes=[pltpu.VMEM((SUBLANES, tn), F32)] + [pltpu.VMEM((d, tn), BF16)] * 3,
        compiler_params=_params(2),
        name="conv_path_prompt",
    )(xn, w_in, w_in, w_in, conv_w, w_in, *slab_weights)


def _conv_path_sample(x2d, g, w_in, conv_w, state_tm, *, n_seq, n_t, d_conv, tn):
    m, d = x2d.shape
    nct = d_conv // tn
    n_state = CONV_WIDTH - 1
    kern = functools.partial(_conv_sample_kernel, n_seq=n_seq, n_t=n_t)
    return pl.pallas_call(
        kern,
        grid=(nct,),
        in_specs=[
            pl.BlockSpec((m, d), lambda j: (0, 0)),
            pl.BlockSpec((1, d), lambda j: (0, 0)),
            pl.BlockSpec((d, tn), lambda j: (0, j)),
            pl.BlockSpec((d, tn), lambda j: (0, nct + j)),
            pl.BlockSpec((d, tn), lambda j: (0, 2 * nct + j)),
            pl.BlockSpec((CONV_WIDTH, tn), lambda j: (0, j)),
            pl.BlockSpec((n_state * n_seq, tn), lambda j: (0, j)),
        ],
        out_specs=[
            pl.BlockSpec((m, d), lambda j: (0, 0)),
            pl.BlockSpec((m, tn), lambda j: (0, j)),
            pl.BlockSpec((n_state * n_seq, tn), lambda j: (0, j)),
        ],
        out_shape=[
            jax.ShapeDtypeStruct((m, d), BF16),
            jax.ShapeDtypeStruct((m, d_conv), BF16),
            jax.ShapeDtypeStruct((n_state * n_seq, d_conv), F32),
        ],
        compiler_params=_params(1),
        name="conv_path_sample",
    )(x2d, g, w_in, w_in, w_in, conv_w, state_tm)


def _pool_prompt_kernel(x_ref, g_ref, wv_ref, wp_ref, ps_ref, xn_ref, mixed_ref, st_ref, carry_ref, wvb_ref,
                        *, tm, blocks_per_seq, d_group):
    i = pl.program_id(0)
    blk = i % blocks_per_seq

    @pl.when(i == 0)
    def _():
        wvb_ref[...] = wv_ref[...].astype(BF16)

    @pl.when(blk == 0)
    def _():
        carry_ref[...] = jnp.zeros(carry_ref.shape, F32)

    xn = _rmsnorm(x_ref[...], g_ref[...]).astype(BF16)
    xn_ref[...] = xn
    pos = blk * tm + lax.broadcasted_iota(jnp.int32, (tm, 1), 0)
    n_groups = len(POOL_WINDOWS)
    for g in sorted(range(n_groups), key=lambda q: (-(q // 2), q)):
        w = POOL_WINDOWS[g]
        lo, hi = g * d_group, (g + 1) * d_group
        if g % 2 == 0:
            v_pair = _dot(xn, wvb_ref[:, lo:lo + 2 * d_group])
        v = v_pair[:, (g % 2) * d_group:(g % 2 + 1) * d_group]
        s = jnp.concatenate([carry_ref[:, lo:hi], v], axis=0)
        tail = v[tm - POOL_CARRY_ROWS:tm]
        carry_ref[:, lo:hi] = tail
        st_ref[0, :, lo:hi] = tail
        k = 1
        while k < w:
            s = s + pltpu.roll(s, k, 0)
            k *= 2
        cnt = jnp.minimum(pos + 1, w).astype(F32)
        pooled = s[POOL_CARRY_ROWS:] / cnt - v
        mixed = _dot(pooled.astype(BF16), wp_ref[g].astype(BF16)) * ps_ref[:, lo:hi]
        mixed_ref[:, lo:hi] = mixed.astype(BF16)


def _pool_sample_kernel(xn_ref, wv_ref, wp_ref, ps_ref, s_ref, mixed_ref, st_ref, *, n_seq, n_t, d_group):
    v = _dot(xn_ref[...], wv_ref[...])
    for g, w in enumerate(POOL_WINDOWS):
        lo, hi = g * d_group, (g + 1) * d_group
        ext = [s_ref[k * n_seq:(k + 1) * n_seq, lo:hi] for k in range(POOL_BUF)]
        ext += [v[t * n_seq:(t + 1) * n_seq, lo:hi] for t in range(n_t)]
        cnt = float(min(PAST_LEN + 1, w))
        wp = wp_ref[g].astype(BF16)
        for t in range(n_t):
            s = ext[POOL_BUF + t]
            for k in range(1, w):
                s = s + ext[POOL_BUF + t - k]
            pooled = s / cnt - ext[POOL_BUF + t]
            mixed = _dot(pooled.astype(BF16), wp) * ps_ref[:, lo:hi]
            mixed_ref[t * n_seq:(t + 1) * n_seq, lo:hi] = mixed.astype(BF16)
        for k in range(POOL_BUF):
            st_ref[k * n_seq:(k + 1) * n_seq, lo:hi] = ext[n_t + k]


def _pool_path_prompt(x2d, g, w_in, w_pool, pool_scale, *, n_batch, seq, d_conv, d_pool, tm):
    m, d = x2d.shape
    n_groups = len(POOL_WINDOWS)
    d_group = d_pool // n_groups
    v_off = 3 * d_conv
    kern = functools.partial(_pool_prompt_kernel, tm=tm, blocks_per_seq=seq // tm, d_group=d_group)
    return pl.pallas_call(
        kern,
        grid=(m // tm,),
        in_specs=[
            pl.BlockSpec((tm, d), lambda i: (i, 0)),
            pl.BlockSpec((1, d), lambda i: (0, 0)),
            pl.BlockSpec((d, d_pool), lambda i: (0, v_off // d_pool)),
            pl.BlockSpec((n_groups, d_group, d_group), lambda i: (0, 0, 0)),
            pl.BlockSpec((1, d_pool), lambda i: (0, 0)),
        ],
        out_specs=[
            pl.BlockSpec((tm, d), lambda i: (i, 0)),
            pl.BlockSpec((tm, d_pool), lambda i: (i, 0)),
            pl.BlockSpec((1, POOL_CARRY_ROWS, d_pool), lambda i: ((i * tm) // seq, 0, 0)),
        ],
        out_shape=[
            jax.ShapeDtypeStruct((m, d), BF16),
            jax.ShapeDtypeStruct((m, d_pool), BF16),
            jax.ShapeDtypeStruct((n_batch, POOL_CARRY_ROWS, d_pool), F32),
        ],
        scratch_shapes=[pltpu.VMEM((POOL_CARRY_ROWS, d_pool), F32), pltpu.VMEM((d, d_pool), BF16)],
        compiler_params=_params(1),
        name="pool_path_prompt",
    )(x2d, g, w_in, w_pool, pool_scale)


def _pool_path_sample(xn, w_tail_bf, w_pool, pool_scale, state_tm, *, n_seq, n_t, d_pool):
    m, d = xn.shape
    n_groups = len(POOL_WINDOWS)
    d_group = d_pool // n_groups
    kern = functools.partial(_pool_sample_kernel, n_seq=n_seq, n_t=n_t, d_group=d_group)
    return pl.pallas_call(
        kern,
        grid=(1,),
        in_specs=[
            pl.BlockSpec((m, d), lambda i: (0, 0)),
            pl.BlockSpec((d, d_pool), lambda i: (0, 0)),
            pl.BlockSpec((n_groups, d_group, d_group), lambda i: (0, 0, 0)),
            pl.BlockSpec((1, d_pool), lambda i: (0, 0)),
            pl.BlockSpec((POOL_BUF * n_seq, d_pool), lambda i: (0, 0)),
        ],
        out_specs=[
            pl.BlockSpec((m, d_pool), lambda i: (0, 0)),
            pl.BlockSpec((POOL_BUF * n_seq, d_pool), lambda i: (0, 0)),
        ],
        out_shape=[
            jax.ShapeDtypeStruct((m, d_pool), BF16),
            jax.ShapeDtypeStruct((POOL_BUF * n_seq, d_pool), F32),
        ],
        compiler_params=_params(1),
        name="pool_path_sample",
    )(xn, w_tail_bf, w_pool, pool_scale, state_tm)


def _gate_merge_kernel(xn_ref, yc_ref, mx_ref, wgc_ref, wgp_ref, wbc_ref, wbp_ref, *rest, n_riders):
    rider_src = rest[:n_riders]
    o_ref = rest[n_riders]
    rider_dst = rest[n_riders + 1:2 * n_riders + 1]
    xn = xn_ref[...]
    gc = _dot(xn, wgc_ref[...])
    gp = _dot(xn, wgp_ref[...])
    y_conv = _dot(yc_ref[...], wbc_ref[...])
    y_pool = _dot(mx_ref[...], wbp_ref[...])
    o_ref[...] = (jax.nn.sigmoid(gc) * y_conv + jax.nn.sigmoid(gp) * y_pool).astype(BF16)
    _cast_riders(rider_src, rider_dst)


def _gate_merge(xn, ycin, mixed, w_tail_bf, w_brc_bf, w_brp_bf, slab_weights, *, gc_off, gp_off, tm, tn):
    m, d = xn.shape
    d_conv = ycin.shape[1]
    d_pool = mixed.shape[1]
    nct = d // tn
    riders = [_slab_rider(w, (m // tm) * nct, lambda i, j: i * nct + j) for w in slab_weights]
    out = pl.pallas_call(
        functools.partial(_gate_merge_kernel, n_riders=len(riders)),
        grid=(m // tm, nct),
        in_specs=[
            pl.BlockSpec((tm, d), lambda i, j: (i, 0)),
            pl.BlockSpec((tm, d_conv), lambda i, j: (i, 0)),
            pl.BlockSpec((tm, d_pool), lambda i, j: (i, 0)),
            pl.BlockSpec((d, tn), lambda i, j: (0, gc_off // tn + j)),
            pl.BlockSpec((d, tn), lambda i, j: (0, gp_off // tn + j)),
            pl.BlockSpec((d_conv, tn), lambda i, j: (0, j)),
            pl.BlockSpec((d_pool, tn), lambda i, j: (0, j)),
        ] + [r[0] for r in riders],
        out_specs=[pl.BlockSpec((tm, tn), lambda i, j: (i, j))] + [r[1] for r in riders],
        out_shape=[jax.ShapeDtypeStruct((m, d), BF16)] + [r[2] for r in riders],
        compiler_params=_params(2),
        name="gate_merge",
    )(xn, ycin, mixed, w_tail_bf, w_tail_bf, w_brc_bf, w_brp_bf, *slab_weights)
    return out[0], out[1:]


def _out_proj_kernel(xp_ref, mp_ref, xs_ref, ms_ref, wo_ref, x1p_ref, x1s_ref, *, n_prompt):
    i = pl.program_id(0)

    @pl.when(i < n_prompt)
    def _():
        x1p_ref[...] = xp_ref[...] + _dot(mp_ref[...], wo_ref[...])

    @pl.when(i == n_prompt)
    def _():
        x1s_ref[...] = xs_ref[...] + _dot(ms_ref[...], wo_ref[...])


def _out_proj(xp, merged_p, xs, merged_s, w_o_bf, *, tm):
    m, d = xp.shape
    n_prompt = m // tm
    assert xs.shape[0] == tm
    last = n_prompt - 1
    row = lambda i: (jnp.minimum(i, last), 0)
    const = lambda i: (0, 0)
    return pl.pallas_call(
        functools.partial(_out_proj_kernel, n_prompt=n_prompt),
        grid=(n_prompt + 1,),
        in_specs=[
            pl.BlockSpec((tm, d), row),
            pl.BlockSpec((tm, d), row),
            pl.BlockSpec((tm, d), const),
            pl.BlockSpec((tm, d), const),
            pl.BlockSpec((d, d), const, pipeline_mode=pl.Buffered(1)),
        ],
        out_specs=[pl.BlockSpec((tm, d), row), pl.BlockSpec((tm, d), const)],
        out_shape=[jax.ShapeDtypeStruct((m, d), F32), jax.ShapeDtypeStruct((tm, d), F32)],
        compiler_params=_params(1),
        name="out_proj_residual",
    )(xp, merged_p, xs, merged_s, w_o_bf)


def _ffn_kernel(xp_ref, xs_ref, wg_ref, wu_ref, wd_ref, gf_ref, gl_ref, op_ref, os_ref, hn_ref, *, final_norm):
    f = pl.program_id(1)
    tm = xp_ref.shape[0]

    @pl.when(f == 0)
    def _():
        xp = xp_ref[...]
        xs = xs_ref[...]
        hn_ref[0:tm, :] = _rmsnorm(xp, gf_ref[...]).astype(BF16)
        hn_ref[tm:, :] = _rmsnorm(xs, gf_ref[...]).astype(BF16)
        op_ref[...] = xp
        os_ref[...] = xs

    hn = hn_ref[...]
    half = wg_ref.shape[1] // 2
    ff = []
    for k in range(2):
        cols = slice(k * half, (k + 1) * half)
        gate = _dot(hn, wg_ref[:, cols])
        up = _dot(hn, wu_ref[:, cols])
        ff.append(((gate * jax.nn.sigmoid(gate)) * up).astype(BF16))
    down = _dot(jnp.concatenate(ff, axis=1), wd_ref[...])
    op_ref[...] += down[0:tm]
    os_ref[...] += down[tm:]

    if final_norm:
        @pl.when(f == pl.num_programs(1) - 1)
        def _():
            op_ref[...] = _rmsnorm(op_ref[...], gl_ref[...])
            os_ref[...] = _rmsnorm(os_ref[...], gl_ref[...])


def _ffn(x1p, x1s, w_gate_bf, w_up_bf, w_down_bf, g_ffn, g_final, *, final_norm, tm, tf):
    m, d = x1p.shape
    d_ff = w_gate_bf.shape[1]
    n_blocks = m // tm
    rs = x1s.shape[0] // n_blocks
    assert rs * n_blocks == x1s.shape[0] and rs % HEAD_ROWS == 0
    kern = functools.partial(_ffn_kernel, final_norm=final_norm)
    return pl.pallas_call(
        kern,
        grid=(n_blocks, d_ff // tf),
        in_specs=[
            pl.BlockSpec((tm, d), lambda i, f: (i, 0)),
            pl.BlockSpec((rs, d), lambda i, f: (i, 0)),
            pl.BlockSpec((d, tf), lambda i, f: (0, f)),
            pl.BlockSpec((d, tf), lambda i, f: (0, f)),
            pl.BlockSpec((tf, d), lambda i, f: (f, 0)),
            pl.BlockSpec((1, d), lambda i, f: (0, 0)),
            pl.BlockSpec((1, d), lambda i, f: (0, 0)),
        ],
        out_specs=[pl.BlockSpec((tm, d), lambda i, f: (i, 0)), pl.BlockSpec((rs, d), lambda i, f: (i, 0))],
        out_shape=[jax.ShapeDtypeStruct((m, d), F32), jax.ShapeDtypeStruct(x1s.shape, F32)],
        scratch_shapes=[pltpu.VMEM((tm + rs, d), BF16)],
        compiler_params=_params(2),
        name="swiglu_ffn",
    )(x1p, x1s, w_gate_bf, w_up_bf, w_down_bf, g_ffn, g_final)


def _tiles(d_conv, d_ff):
    tm = 1024
    tm_pool = 512
    tn_conv = 512 if d_conv % 512 == 0 else 256
    tn_merge = 512
    tm_out = 512
    tf = 512 if d_ff % 512 == 0 else 256
    return tm, tm_pool, tm_out, tn_conv, tn_merge, tf


def kernel(x_prompt, x_sample, state_conv, state_pool, norm_mix, w_in, conv_w, w_pool, pool_scale, w_br_conv,
           w_br_pool, w_o, norm_ffn, w_gate, w_up, w_down, norm_final):
    n_batch, seq, d = x_prompt.shape
    n_seq, n_t, _ = x_sample.shape
    depth = w_in.shape[0]
    d_conv = conv_w.shape[2]
    d_pool = pool_scale.shape[1]
    d_ff = w_gate.shape[2]
    gc_off = d_pool
    gp_off = gc_off + d
    tm, tm_pool, tm_out, tn_conv, tn_merge, tf = _tiles(d_conv, d_ff)
    assert seq % tm == 0 and seq % tm_pool == 0 and tm_pool >= POOL_CARRY_ROWS and n_t >= CONV_WIDTH - 1
    assert (n_batch * seq) % tm_out == 0 and n_t * n_seq == tm_out
    assert n_seq % 16 == 0 and gc_off % tn_merge == 0 and gp_off % tn_merge == 0

    yp = x_prompt.reshape(n_batch * seq, d)
    ys = jnp.transpose(x_sample, (1, 0, 2)).reshape(n_t * n_seq, d)
    g_final = norm_final.reshape(1, d)
    conv_p, pool_p, conv_s, pool_s = [], [], [], []
    for l in range(depth):
        g_mix = norm_mix[l].reshape(1, d)
        g_ffn = norm_ffn[l].reshape(1, d)
        ps = pool_scale[l].reshape(1, d_pool)
        last = l == depth - 1

        xn, mixed, st_p = _pool_path_prompt(yp, g_mix, w_in[l], w_pool[l], ps, n_batch=n_batch, seq=seq,
                                            d_conv=d_conv, d_pool=d_pool, tm=tm_pool)
        ycin, st_c, w_tail_bf, w_brc_bf, w_brp_bf = _conv_path_prompt(
            xn, conv_w[l], w_in[l], (w_br_conv[l], w_br_pool[l]), seq=seq, d_conv=d_conv, tm=tm, tn=tn_conv)
        merged, (w_o_bf, w_down_bf, w_gate_bf, w_up_bf) = _gate_merge(
            xn, ycin, mixed, w_tail_bf, w_brc_bf, w_brp_bf, (w_o[l], w_down[l], w_gate[l], w_up[l]),
            gc_off=gc_off, gp_off=gp_off, tm=tm, tn=tn_merge)
        bps = seq // tm
        conv_p.append(st_c[bps - 1::bps, SUBLANES - (CONV_WIDTH - 1):, :])
        pool_p.append(st_p[:, POOL_CARRY_ROWS - POOL_BUF:, :])

        sc_tm = jnp.transpose(state_conv[l], (1, 0, 2)).reshape((CONV_WIDTH - 1) * n_seq, d_conv)
        sp_tm = jnp.transpose(state_pool[l], (1, 0, 2)).reshape(POOL_BUF * n_seq, d_pool)
        xn_s, ycin_s, st_c = _conv_path_sample(ys, g_mix, w_in[l], conv_w[l], sc_tm, n_seq=n_seq, n_t=n_t,
                                               d_conv=d_conv, tn=tn_conv)
        mixed_s, st_p = _pool_path_sample(xn_s, w_tail_bf, w_pool[l], ps, sp_tm, n_seq=n_seq, n_t=n_t,
                                          d_pool=d_pool)
        merged_s, _ = _gate_merge(xn_s, ycin_s, mixed_s, w_tail_bf, w_brc_bf, w_brp_bf, (),
                                  gc_off=gc_off, gp_off=gp_off, tm=n_t * n_seq, tn=tn_merge)
        conv_s.append(jnp.transpose(st_c.reshape(CONV_WIDTH - 1, n_seq, d_conv), (1, 0, 2)))
        pool_s.append(jnp.transpose(st_p.reshape(POOL_BUF, n_seq, d_pool), (1, 0, 2)))

        x1p, x1s = _out_proj(yp, merged, ys, merged_s, w_o_bf, tm=tm_out)
        yp, ys = _ffn(x1p, x1s, w_gate_bf, w_up_bf, w_down_bf, g_ffn, g_final, final_norm=last, tm=tm, tf=tf)

    y_prompt = yp.reshape(n_batch, seq, d)
    y_sample = jnp.transpose(ys.reshape(n_t, n_seq, d), (1, 0, 2))
    return (y_prompt, y_sample, jnp.stack(conv_p, axis=0), jnp.stack(pool_p, axis=0),
            jnp.stack(conv_s, axis=0), jnp.stack(pool_s, axis=0))
```

```python
import functools

import jax
import jax.numpy as jnp
from jax import lax
from jax.experimental import pallas as pl
from jax.experimental.pallas import tpu as pltpu

EPS = 1e-6
CONV_WIDTH = 3
POOL_WINDOWS = (2, 4, 8, 16)
POOL_BUF = max(POOL_WINDOWS) - 1
PAST_LEN = 16384

SUBLANES = 8
HEAD_ROWS = 16
POOL_CARRY_ROWS = 16
VMEM_LIMIT_BYTES = 56 * 1024 * 1024

BF16 = jnp.bfloat16
F32 = jnp.float32


def _dot(a, b):
    return jnp.dot(a, b, preferred_element_type=F32)


def _rmsnorm(x, g):
    r = lax.rsqrt(jnp.mean(x * x, axis=-1, keepdims=True) + EPS)
    return (x * r) * g


def _params(n_axes):
    return pltpu.CompilerParams(dimension_semantics=("arbitrary",) * n_axes,
                                vmem_limit_bytes=VMEM_LIMIT_BYTES)


def _cast_riders(src_refs, dst_refs):
    for src, dst in zip(src_refs, dst_refs):
        dst[...] = src[...].astype(BF16)


def _slab_rider(w, n_steps, step_of):
    rows, cols = w.shape
    slab = rows // n_steps
    assert slab * n_steps == rows and slab % HEAD_ROWS == 0
    spec = pl.BlockSpec((slab, cols), lambda *ids: (step_of(*ids), 0))
    return spec, spec, jax.ShapeDtypeStruct((rows, cols), BF16)


def _col_rider(w, col0, col1, n_steps, step_of, *, width):
    rows = w.shape[0]
    n_col = (col1 - col0) // width
    assert n_col * width == col1 - col0 and col0 % width == 0 and n_col <= n_steps
    n_row = 1
    while n_row * 2 * n_col <= n_steps and rows % (n_row * 2) == 0:
        n_row *= 2
    rb = rows // n_row
    assert rb % HEAD_ROWS == 0
    n_blk = n_row * n_col

    def blk(*ids):
        return jnp.minimum(step_of(*ids), n_blk - 1)

    in_spec = pl.BlockSpec((rb, width), lambda *ids: (blk(*ids) // n_col, col0 // width + blk(*ids) % n_col))
    out_spec = pl.BlockSpec((rb, width), lambda *ids: (blk(*ids) // n_col, blk(*ids) % n_col))
    return in_spec, out_spec, jax.ShapeDtypeStruct((rows, col1 - col0), BF16)


def _conv_prompt_kernel(xn_ref, wh_ref, wb_ref, wc_ref, cw_ref, *rest, tm, blocks_per_seq, n_riders):
    rider_src = rest[:n_riders]
    y_ref, st_ref = rest[n_riders:n_riders + 2]
    rider_dst = rest[n_riders + 2:2 * n_riders + 2]
    carry_ref, whb_ref, wbb_ref, wcb_ref = rest[2 * n_riders + 2:]
    i = pl.program_id(1)

    @pl.when(i == 0)
    def _():
        whb_ref[...] = wh_ref[...].astype(BF16)
        wbb_ref[...] = wb_ref[...].astype(BF16)
        wcb_ref[...] = wc_ref[...].astype(BF16)

    @pl.when(i % blocks_per_seq == 0)
    def _():
        carry_ref[...] = jnp.zeros(carry_ref.shape, F32)

    xn = xn_ref[...]
    c = _dot(xn, wcb_ref[...])
    h = _dot(xn, whb_ref[...])
    u = c * h
    b = _dot(xn, wbb_ref[...])
    w0 = cw_ref[0:1, :]
    w1 = cw_ref[1:2, :]
    w2 = cw_ref[2:3, :]
    conv = w0 * pltpu.roll(u, 2, 0) + w1 * pltpu.roll(u, 1, 0) + w2 * u
    y_ref[...] = (b * conv).astype(BF16)
    head = jnp.concatenate([carry_ref[...], u[0:HEAD_ROWS]], axis=0)
    conv_head = w0 * pltpu.roll(head, 2, 0) + w1 * pltpu.roll(head, 1, 0) + w2 * head
    y_ref[0:HEAD_ROWS, :] = (b[0:HEAD_ROWS] * conv_head[SUBLANES:]).astype(BF16)
    tail = u[tm - SUBLANES:tm]
    carry_ref[...] = tail
    st_ref[0] = tail
    _cast_riders(rider_src, rider_dst)


def _conv_sample_kernel(x_ref, g_ref, wh_ref, wb_ref, wc_ref, cw_ref, s_ref, xn_ref, y_ref, st_ref,
                        *, n_seq, n_t):
    j = pl.program_id(0)

    @pl.when(j == 0)
    def _():
        xn_ref[...] = _rmsnorm(x_ref[...], g_ref[...]).astype(BF16)

    xn = xn_ref[...]
    c = _dot(xn, wc_ref[...].astype(BF16))
    h = _dot(xn, wh_ref[...].astype(BF16))
    u = c * h
    b = _dot(xn, wb_ref[...].astype(BF16))
    w0 = cw_ref[0:1, :]
    w1 = cw_ref[1:2, :]
    w2 = cw_ref[2:3, :]
    n_state = CONV_WIDTH - 1
    ext = [s_ref[k * n_seq:(k + 1) * n_seq, :] for k in range(n_state)]
    ext += [u[t * n_seq:(t + 1) * n_seq] for t in range(n_t)]
    for t in range(n_t):
        conv = w0 * ext[t] + w1 * ext[t + 1] + w2 * ext[t + 2]
        y_ref[t * n_seq:(t + 1) * n_seq, :] = (b[t * n_seq:(t + 1) * n_seq] * conv).astype(BF16)
    for k in range(n_state):
        st_ref[k * n_seq:(k + 1) * n_seq, :] = ext[n_t + k]


def _conv_path_prompt(xn, conv_w, w_in, slab_weights, *, seq, d_conv, tm, tn):
    m, d = xn.shape
    nct = d_conv // tn
    n_i = m // tm
    n_steps = nct * n_i
    step_of = lambda j, i: j * n_i + i
    riders = [_col_rider(w_in, 3 * d_conv, w_in.shape[1], n_steps, step_of, width=tn * 2)]
    riders += [_slab_rider(w, n_steps, step_of) for w in slab_weights]
    kern = functools.partial(_conv_prompt_kernel, tm=tm, blocks_per_seq=seq // tm, n_riders=len(riders))
    return pl.pallas_call(
        kern,
        grid=(nct, n_i),
        in_specs=[
            pl.BlockSpec((tm, d), lambda j, i: (i, 0)),
            pl.BlockSpec((d, tn), lambda j, i: (0, j)),
            pl.BlockSpec((d, tn), lambda j, i: (0, nct + j)),
            pl.BlockSpec((d, tn), lambda j, i: (0, 2 * nct + j)),
            pl.BlockSpec((CONV_WIDTH, tn), lambda j, i: (0, j)),
        ] + [r[0] for r in riders],
        out_specs=[
            pl.BlockSpec((tm, tn), lambda j, i: (i, j)),
            pl.BlockSpec((1, SUBLANES, tn), lambda j, i: (i, 0, j)),
        ] + [r[1] for r in riders],
        out_shape=[
            jax.ShapeDtypeStruct((m, d_conv), BF16),
            jax.ShapeDtypeStruct((n_i, SUBLANES, d_conv), F32),
        ] + [r[2] for r in riders],
        scratch_shapes=[pltpu.VMEM((SUBLANES, tn), F32)] + [pltpu.VMEM((d, tn), BF16)] * 3,
        compiler_params=_params(2),
        name="conv_path_prompt",
    )(xn, w_in, w_in, w_in, conv_w, w_in, *slab_weights)


def _conv_path_sample(x2d, g, w_in, conv_w, state_tm, *, n_seq, n_t, d_conv, tn):
    m, d = x2d.shape
    nct = d_conv // tn
    n_state = CONV_WIDTH - 1
    kern = functools.partial(_conv_sample_kernel, n_seq=n_seq, n_t=n_t)
    return pl.pallas_call(
        kern,
        grid=(nct,),
        in_specs=[
            pl.BlockSpec((m, d), lambda j: (0, 0)),
            pl.BlockSpec((1, d), lambda j: (0, 0)),
            pl.BlockSpec((d, tn), lambda j: (0, j)),
            pl.BlockSpec((d, tn), lambda j: (0, nct + j)),
            pl.BlockSpec((d, tn), lambda j: (0, 2 * nct + j)),
            pl.BlockSpec((CONV_WIDTH, tn), lambda j: (0, j)),
            pl.BlockSpec((n_state * n_seq, tn), lambda j: (0, j)),
        ],
        out_specs=[
            pl.BlockSpec((m, d), lambda j: (0, 0)),
            pl.BlockSpec((m, tn), lambda j: (0, j)),
            pl.BlockSpec((n_state * n_seq, tn), lambda j: (0, j)),
        ],
        out_shape=[
            jax.ShapeDtypeStruct((m, d), BF16),
            jax.ShapeDtypeStruct((m, d_conv), BF16),
            jax.ShapeDtypeStruct((n_state * n_seq, d_conv), F32),
        ],
        compiler_params=_params(1),
        name="conv_path_sample",
    )(x2d, g, w_in, w_in, w_in, conv_w, state_tm)


def _pool_prompt_kernel(x_ref, g_ref, wv_ref, wp_ref, ps_ref, xn_ref, mixed_ref, st_ref, carry_ref, wvb_ref,
                        *, tm, blocks_per_seq, d_group):
    i = pl.program_id(0)
    blk = i % blocks_per_seq

    @pl.when(i == 0)
    def _():
        wvb_ref[...] = wv_ref[...].astype(BF16)

    @pl.when(blk == 0)
    def _():
        carry_ref[...] = jnp.zeros(carry_ref.shape, F32)

    xn = _rmsnorm(x_ref[...], g_ref[...]).astype(BF16)
    xn_ref[...] = xn
    pos = blk * tm + lax.broadcasted_iota(jnp.int32, (tm, 1), 0)
    n_groups = len(POOL_WINDOWS)
    order = sorted(range(n_groups), key=lambda q: (-(q // 2), q))
    v_pairs = {g // 2: _dot(xn, wvb_ref[:, g * d_group:(g + 2) * d_group]) for g in order if g % 2 == 0}
    for g in order:
        w = POOL_WINDOWS[g]
        lo, hi = g * d_group, (g + 1) * d_group
        v = v_pairs[g // 2][:, (g % 2) * d_group:(g % 2 + 1) * d_group]
        s = jnp.concatenate([carry_ref[:, lo:hi], v], axis=0)
        tail = v[tm - POOL_CARRY_ROWS:tm]
        carry_ref[:, lo:hi] = tail
        st_ref[0, :, lo:hi] = tail
        k = 1
        while k < w:
            s = s + pltpu.roll(s, k, 0)
            k *= 2
        cnt = jnp.minimum(pos + 1, w).astype(F32)
        pooled = s[POOL_CARRY_ROWS:] / cnt - v
        mixed = _dot(pooled.astype(BF16), wp_ref[g].astype(BF16)) * ps_ref[:, lo:hi]
        mixed_ref[:, lo:hi] = mixed.astype(BF16)


def _pool_sample_kernel(xn_ref, wv_ref, wp_ref, ps_ref, s_ref, mixed_ref, st_ref, *, n_seq, n_t, d_group):
    v = _dot(xn_ref[...], wv_ref[...])
    for g, w in enumerate(POOL_WINDOWS):
        lo, hi = g * d_group, (g + 1) * d_group
        ext = [s_ref[k * n_seq:(k + 1) * n_seq, lo:hi] for k in range(POOL_BUF)]
        ext += [v[t * n_seq:(t + 1) * n_seq, lo:hi] for t in range(n_t)]
        cnt = float(min(PAST_LEN + 1, w))
        wp = wp_ref[g].astype(BF16)
        for t in range(n_t):
            s = ext[POOL_BUF + t]
            for k in range(1, w):
                s = s + ext[POOL_BUF + t - k]
            pooled = s / cnt - ext[POOL_BUF + t]
            mixed = _dot(pooled.astype(BF16), wp) * ps_ref[:, lo:hi]
            mixed_ref[t * n_seq:(t + 1) * n_seq, lo:hi] = mixed.astype(BF16)
        for k in range(POOL_BUF):
            st_ref[k * n_seq:(k + 1) * n_seq, lo:hi] = ext[n_t + k]


def _pool_path_prompt(x2d, g, w_in, w_pool, pool_scale, *, n_batch, seq, d_conv, d_pool, tm):
    m, d = x2d.shape
    n_groups = len(POOL_WINDOWS)
    d_group = d_pool // n_groups
    v_off = 3 * d_conv
    kern = functools.partial(_pool_prompt_kernel, tm=tm, blocks_per_seq=seq // tm, d_group=d_group)
    return pl.pallas_call(
        kern,
        grid=(m // tm,),
        in_specs=[
            pl.BlockSpec((tm, d), lambda i: (i, 0)),
            pl.BlockSpec((1, d), lambda i: (0, 0)),
            pl.BlockSpec((d, d_pool), lambda i: (0, v_off // d_pool)),
            pl.BlockSpec((n_groups, d_group, d_group), lambda i: (0, 0, 0)),
            pl.BlockSpec((1, d_pool), lambda i: (0, 0)),
        ],
        out_specs=[
            pl.BlockSpec((tm, d), lambda i: (i, 0)),
            pl.BlockSpec((tm, d_pool), lambda i: (i, 0)),
            pl.BlockSpec((1, POOL_CARRY_ROWS, d_pool), lambda i: ((i * tm) // seq, 0, 0)),
        ],
        out_shape=[
            jax.ShapeDtypeStruct((m, d), BF16),
            jax.ShapeDtypeStruct((m, d_pool), BF16),
            jax.ShapeDtypeStruct((n_batch, POOL_CARRY_ROWS, d_pool), F32),
        ],
        scratch_shapes=[pltpu.VMEM((POOL_CARRY_ROWS, d_pool), F32), pltpu.VMEM((d, d_pool), BF16)],
        compiler_params=_params(1),
        name="pool_path_prompt",
    )(x2d, g, w_in, w_pool, pool_scale)


def _pool_path_sample(xn, w_tail_bf, w_pool, pool_scale, state_tm, *, n_seq, n_t, d_pool):
    m, d = xn.shape
    n_groups = len(POOL_WINDOWS)
    d_group = d_pool // n_groups
    kern = functools.partial(_pool_sample_kernel, n_seq=n_seq, n_t=n_t, d_group=d_group)
    return pl.pallas_call(
        kern,
        grid=(1,),
        in_specs=[
            pl.BlockSpec((m, d), lambda i: (0, 0)),
            pl.BlockSpec((d, d_pool), lambda i: (0, 0)),
            pl.BlockSpec((n_groups, d_group, d_group), lambda i: (0, 0, 0)),
            pl.BlockSpec((1, d_pool), lambda i: (0, 0)),
            pl.BlockSpec((POOL_BUF * n_seq, d_pool), lambda i: (0, 0)),
        ],
        out_specs=[
            pl.BlockSpec((m, d_pool), lambda i: (0, 0)),
            pl.BlockSpec((POOL_BUF * n_seq, d_pool), lambda i: (0, 0)),
        ],
        out_shape=[
            jax.ShapeDtypeStruct((m, d_pool), BF16),
            jax.ShapeDtypeStruct((POOL_BUF * n_seq, d_pool), F32),
        ],
        compiler_params=_params(1),
        name="pool_path_sample",
    )(xn, w_tail_bf, w_pool, pool_scale, state_tm)


def _gate_merge_kernel(xn_ref, yc_ref, mx_ref, wgc_ref, wgp_ref, wbc_ref, wbp_ref, *rest, n_riders):
    rider_src = rest[:n_riders]
    o_ref = rest[n_riders]
    rider_dst = rest[n_riders + 1:2 * n_riders + 1]
    xn = xn_ref[...]
    gc = _dot(xn, wgc_ref[...])
    gp = _dot(xn, wgp_ref[...])
    y_conv = _dot(yc_ref[...], wbc_ref[...])
    y_pool = _dot(mx_ref[...], wbp_ref[...])
    o_ref[...] = (jax.nn.sigmoid(gc) * y_conv + jax.nn.sigmoid(gp) * y_pool).astype(BF16)
    _cast_riders(rider_src, rider_dst)


def _gate_merge(xn, ycin, mixed, w_tail_bf, w_brc_bf, w_brp_bf, slab_weights, *, gc_off, gp_off, tm, tn):
    m, d = xn.shape
    d_conv = ycin.shape[1]
    d_pool = mixed.shape[1]
    nct = d // tn
    riders = [_slab_rider(w, (m // tm) * nct, lambda i, j: i * nct + j) for w in slab_weights]
    out = pl.pallas_call(
        functools.partial(_gate_merge_kernel, n_riders=len(riders)),
        grid=(m // tm, nct),
        in_specs=[
            pl.BlockSpec((tm, d), lambda i, j: (i, 0)),
            pl.BlockSpec((tm, d_conv), lambda i, j: (i, 0)),
            pl.BlockSpec((tm, d_pool), lambda i, j: (i, 0)),
            pl.BlockSpec((d, tn), lambda i, j: (0, gc_off // tn + j)),
            pl.BlockSpec((d, tn), lambda i, j: (0, gp_off // tn + j)),
            pl.BlockSpec((d_conv, tn), lambda i, j: (0, j)),
            pl.BlockSpec((d_pool, tn), lambda i, j: (0, j)),
        ] + [r[0] for r in riders],
        out_specs=[pl.BlockSpec((tm, tn), lambda i, j: (i, j))] + [r[1] for r in riders],
        out_shape=[jax.ShapeDtypeStruct((m, d), BF16)] + [r[2] for r in riders],
        compiler_params=_params(2),
        name="gate_merge",
    )(xn, ycin, mixed, w_tail_bf, w_tail_bf, w_brc_bf, w_brp_bf, *slab_weights)
    return out[0], out[1:]


def _out_proj_kernel(xp_ref, mp_ref, xs_ref, ms_ref, wo_ref, x1p_ref, x1s_ref, *, n_prompt):
    i = pl.program_id(0)

    @pl.when(i < n_prompt)
    def _():
        x1p_ref[...] = xp_ref[...] + _dot(mp_ref[...], wo_ref[...])

    @pl.when(i == n_prompt)
    def _():
        x1s_ref[...] = xs_ref[...] + _dot(ms_ref[...], wo_ref[...])


def _out_proj(xp, merged_p, xs, merged_s, w_o_bf, *, tm):
    m, d = xp.shape
    n_prompt = m // tm
    assert xs.shape[0] == tm
    last = n_prompt - 1
    row = lambda i: (jnp.minimum(i, last), 0)
    const = lambda i: (0, 0)
    return pl.pallas_call(
        functools.partial(_out_proj_kernel, n_prompt=n_prompt),
        grid=(n_prompt + 1,),
        in_specs=[
            pl.BlockSpec((tm, d), row),
            pl.BlockSpec((tm, d), row),
            pl.BlockSpec((tm, d), const),
            pl.BlockSpec((tm, d), const),
            pl.BlockSpec((d, d), const, pipeline_mode=pl.Buffered(1)),
        ],
        out_specs=[pl.BlockSpec((tm, d), row), pl.BlockSpec((tm, d), const)],
        out_shape=[jax.ShapeDtypeStruct((m, d), F32), jax.ShapeDtypeStruct((tm, d), F32)],
        compiler_params=_params(1),
        name="out_proj_residual",
    )(xp, merged_p, xs, merged_s, w_o_bf)


def _ffn_kernel(xp_ref, xs_ref, wg_ref, wu_ref, wd_ref, gf_ref, gl_ref, op_ref, os_ref, hn_ref, *, final_norm):
    f = pl.program_id(1)
    tm = xp_ref.shape[0]

    @pl.when(f == 0)
    def _():
        xp = xp_ref[...]
        xs = xs_ref[...]
        hn_ref[0:tm, :] = _rmsnorm(xp, gf_ref[...]).astype(BF16)
        hn_ref[tm:, :] = _rmsnorm(xs, gf_ref[...]).astype(BF16)
        op_ref[...] = xp
        os_ref[...] = xs

    hn = hn_ref[...]
    half = wg_ref.shape[1] // 2
    ff = []
    for k in range(2):
        cols = slice(k * half, (k + 1) * half)
        gate = _dot(hn, wg_ref[:, cols])
        up = _dot(hn, wu_ref[:, cols])
        ff.append(((gate * jax.nn.sigmoid(gate)) * up).astype(BF16))
    down = _dot(jnp.concatenate(ff, axis=1), wd_ref[...])
    op_ref[...] += down[0:tm]
    os_ref[...] += down[tm:]

    if final_norm:
        @pl.when(f == pl.num_programs(1) - 1)
        def _():
            op_ref[...] = _rmsnorm(op_ref[...], gl_ref[...])
            os_ref[...] = _rmsnorm(os_ref[...], gl_ref[...])


def _ffn(x1p, x1s, w_gate_bf, w_up_bf, w_down_bf, g_ffn, g_final, *, final_norm, tm, tf):
    m, d = x1p.shape
    d_ff = w_gate_bf.shape[1]
    n_blocks = m // tm
    rs = x1s.shape[0] // n_blocks
    assert rs * n_blocks == x1s.shape[0] and rs % HEAD_ROWS == 0
    kern = functools.partial(_ffn_kernel, final_norm=final_norm)
    return pl.pallas_call(
        kern,
        grid=(n_blocks, d_ff // tf),
        in_specs=[
            pl.BlockSpec((tm, d), lambda i, f: (i, 0)),
            pl.BlockSpec((rs, d), lambda i, f: (i, 0)),
            pl.BlockSpec((d, tf), lambda i, f: (0, f)),
            pl.BlockSpec((d, tf), lambda i, f: (0, f)),
            pl.BlockSpec((tf, d), lambda i, f: (f, 0)),
            pl.BlockSpec((1, d), lambda i, f: (0, 0)),
            pl.BlockSpec((1, d), lambda i, f: (0, 0)),
        ],
        out_specs=[pl.BlockSpec((tm, d), lambda i, f: (i, 0)), pl.BlockSpec((rs, d), lambda i, f: (i, 0))],
        out_shape=[jax.ShapeDtypeStruct((m, d), F32), jax.ShapeDtypeStruct(x1s.shape, F32)],
        scratch_shapes=[pltpu.VMEM((tm + rs, d), BF16)],
        compiler_params=_params(2),
        name="swiglu_ffn",
    )(x1p, x1s, w_gate_bf, w_up_bf, w_down_bf, g_ffn, g_final)


def _tiles(d_conv, d_ff):
    tm = 1024
    tm_pool = 512
    tn_conv = 512 if d_conv % 512 == 0 else 256
    tn_merge = 512
    tm_out = 512
    tf = 512 if d_ff % 512 == 0 else 256
    return tm, tm_pool, tm_out, tn_conv, tn_merge, tf


def kernel(x_prompt, x_sample, state_conv, state_pool, norm_mix, w_in, conv_w, w_pool, pool_scale, w_br_conv,
           w_br_pool, w_o, norm_ffn, w_gate, w_up, w_down, norm_final):
    n_batch, seq, d = x_prompt.shape
    n_seq, n_t, _ = x_sample.shape
    depth = w_in.shape[0]
    d_conv = conv_w.shape[2]
    d_pool = pool_scale.shape[1]
    d_ff = w_gate.shape[2]
    gc_off = d_pool
    gp_off = gc_off + d
    tm, tm_pool, tm_out, tn_conv, tn_merge, tf = _tiles(d_conv, d_ff)
    assert seq % tm == 0 and seq % tm_pool == 0 and tm_pool >= POOL_CARRY_ROWS and n_t >= CONV_WIDTH - 1
    assert (n_batch * seq) % tm_out == 0 and n_t * n_seq == tm_out
    assert n_seq % 16 == 0 and gc_off % tn_merge == 0 and gp_off % tn_merge == 0

    yp = x_prompt.reshape(n_batch * seq, d)
    ys = jnp.transpose(x_sample, (1, 0, 2)).reshape(n_t * n_seq, d)
    g_final = norm_final.reshape(1, d)
    conv_p, pool_p, conv_s, pool_s = [], [], [], []
    for l in range(depth):
        g_mix = norm_mix[l].reshape(1, d)
        g_ffn = norm_ffn[l].reshape(1, d)
        ps = pool_scale[l].reshape(1, d_pool)
        last = l == depth - 1

        xn, mixed, st_p = _pool_path_prompt(yp, g_mix, w_in[l], w_pool[l], ps, n_batch=n_batch, seq=seq,
                                            d_conv=d_conv, d_pool=d_pool, tm=tm_pool)
        ycin, st_c, w_tail_bf, w_brc_bf, w_brp_bf = _conv_path_prompt(
            xn, conv_w[l], w_in[l], (w_br_conv[l], w_br_pool[l]), seq=seq, d_conv=d_conv, tm=tm, tn=tn_conv)
        merged, (w_o_bf, w_down_bf, w_gate_bf, w_up_bf) = _gate_merge(
            xn, ycin, mixed, w_tail_bf, w_brc_bf, w_brp_bf, (w_o[l], w_down[l], w_gate[l], w_up[l]),
            gc_off=gc_off, gp_off=gp_off, tm=tm, tn=tn_merge)
        bps = seq // tm
        conv_p.append(st_c[bps - 1::bps, SUBLANES - (CONV_WIDTH - 1):, :])
        pool_p.append(st_p[:, POOL_CARRY_ROWS - POOL_BUF:, :])

        sc_tm = jnp.transpose(state_conv[l], (1, 0, 2)).reshape((CONV_WIDTH - 1) * n_seq, d_conv)
        sp_tm = jnp.transpose(state_pool[l], (1, 0, 2)).reshape(POOL_BUF * n_seq, d_pool)
        xn_s, ycin_s, st_c = _conv_path_sample(ys, g_mix, w_in[l], conv_w[l], sc_tm, n_seq=n_seq, n_t=n_t,
                                               d_conv=d_conv, tn=tn_conv)
        mixed_s, st_p = _pool_path_sample(xn_s, w_tail_bf, w_pool[l], ps, sp_tm, n_seq=n_seq, n_t=n_t,
                                          d_pool=d_pool)
        merged_s, _ = _gate_merge(xn_s, ycin_s, mixed_s, w_tail_bf, w_brc_bf, w_brp_bf, (),
                                  gc_off=gc_off, gp_off=gp_off, tm=n_t * n_seq, tn=tn_merge)
        conv_s.append(jnp.transpose(st_c.reshape(CONV_WIDTH - 1, n_seq, d_conv), (1, 0, 2)))
        pool_s.append(jnp.transpose(st_p.reshape(POOL_BUF, n_seq, d_pool), (1, 0, 2)))

        x1p, x1s = _out_proj(yp, merged, ys, merged_s, w_o_bf, tm=tm_out)
        yp, ys = _ffn(x1p, x1s, w_gate_bf, w_up_bf, w_down_bf, g_ffn, g_final, final_norm=last, tm=tm, tf=tf)

    y_prompt = yp.reshape(n_batch, seq, d)
    y_sample = jnp.transpose(ys.reshape(n_t, n_seq, d), (1, 0, 2))
    return (y_prompt, y_sample, jnp.stack(conv_p, axis=0), jnp.stack(pool_p, axis=0),
            jnp.stack(conv_s, axis=0), jnp.stack(pool_s, axis=0))
```

```python
import functools

import jax
import jax.numpy as jnp
from jax import lax
from jax.experimental import pallas as pl
from jax.experimental.pallas import tpu as pltpu

EPS = 1e-6
CONV_WIDTH = 3
POOL_WINDOWS = (2, 4, 8, 16)
POOL_BUF = max(POOL_WINDOWS) - 1
PAST_LEN = 16384

SUBLANES = 8
HEAD_ROWS = 16
POOL_CARRY_ROWS = 16
VMEM_LIMIT_BYTES = 56 * 1024 * 1024

BF16 = jnp.bfloat16
F32 = jnp.float32


def _dot(a, b):
    return jnp.dot(a, b, preferred_element_type=F32)


def _rmsnorm(x, g):
    r = lax.rsqrt(jnp.mean(x * x, axis=-1, keepdims=True) + EPS)
    return (x * r) * g


def _params(n_axes):
    return pltpu.CompilerParams(dimension_semantics=("arbitrary",) * n_axes,
                                vmem_limit_bytes=VMEM_LIMIT_BYTES)


def _cast_riders(src_refs, dst_refs):
    for src, dst in zip(src_refs, dst_refs):
        dst[...] = src[...].astype(BF16)


def _slab_rider(w, n_steps, step_of):
    rows, cols = w.shape
    slab = rows // n_steps
    assert slab * n_steps == rows and slab % HEAD_ROWS == 0
    spec = pl.BlockSpec((slab, cols), lambda *ids: (step_of(*ids), 0))
    return spec, spec, jax.ShapeDtypeStruct((rows, cols), BF16)


def _col_rider(w, col0, col1, n_steps, step_of, *, width):
    rows = w.shape[0]
    n_col = (col1 - col0) // width
    assert n_col * width == col1 - col0 and col0 % width == 0 and n_col <= n_steps
    n_row = 1
    while n_row * 2 * n_col <= n_steps and rows % (n_row * 2) == 0:
        n_row *= 2
    rb = rows // n_row
    assert rb % HEAD_ROWS == 0
    n_blk = n_row * n_col

    def blk(*ids):
        return jnp.minimum(step_of(*ids), n_blk - 1)

    in_spec = pl.BlockSpec((rb, width), lambda *ids: (blk(*ids) // n_col, col0 // width + blk(*ids) % n_col))
    out_spec = pl.BlockSpec((rb, width), lambda *ids: (blk(*ids) // n_col, blk(*ids) % n_col))
    return in_spec, out_spec, jax.ShapeDtypeStruct((rows, col1 - col0), BF16)


def _conv_prompt_kernel(xn_ref, wh_ref, wb_ref, wc_ref, cw_ref, *rest, tm, blocks_per_seq, n_riders):
    rider_src = rest[:n_riders]
    y_ref, st_ref, whb_ref, wbb_ref, wcb_ref = rest[n_riders:n_riders + 5]
    rider_dst = rest[n_riders + 5:2 * n_riders + 5]
    carry_ref = rest[2 * n_riders + 5]
    i = pl.program_id(1)

    @pl.when(i == 0)
    def _():
        whb_ref[...] = wh_ref[...].astype(BF16)
        wbb_ref[...] = wb_ref[...].astype(BF16)
        wcb_ref[...] = wc_ref[...].astype(BF16)

    @pl.when(i % blocks_per_seq == 0)
    def _():
        carry_ref[...] = jnp.zeros(carry_ref.shape, F32)

    xn = xn_ref[...]
    c = _dot(xn, wcb_ref[...])
    h = _dot(xn, whb_ref[...])
    u = c * h
    b = _dot(xn, wbb_ref[...])
    w0 = cw_ref[0:1, :]
    w1 = cw_ref[1:2, :]
    w2 = cw_ref[2:3, :]
    conv = w0 * pltpu.roll(u, 2, 0) + w1 * pltpu.roll(u, 1, 0) + w2 * u
    y_ref[...] = (b * conv).astype(BF16)
    head = jnp.concatenate([carry_ref[...], u[0:HEAD_ROWS]], axis=0)
    conv_head = w0 * pltpu.roll(head, 2, 0) + w1 * pltpu.roll(head, 1, 0) + w2 * head
    y_ref[0:HEAD_ROWS, :] = (b[0:HEAD_ROWS] * conv_head[SUBLANES:]).astype(BF16)
    tail = u[tm - SUBLANES:tm]
    carry_ref[...] = tail
    st_ref[0] = tail
    _cast_riders(rider_src, rider_dst)


def _conv_sample_kernel(x_ref, g_ref, wh_ref, wb_ref, wc_ref, cw_ref, s_ref, xn_ref, y_ref, st_ref,
                        *, n_seq, n_t):
    j = pl.program_id(0)

    @pl.when(j == 0)
    def _():
        xn_ref[...] = _rmsnorm(x_ref[...], g_ref[...]).astype(BF16)

    xn = xn_ref[...]
    c = _dot(xn, wc_ref[...])
    h = _dot(xn, wh_ref[...])
    u = c * h
    b = _dot(xn, wb_ref[...])
    w0 = cw_ref[0:1, :]
    w1 = cw_ref[1:2, :]
    w2 = cw_ref[2:3, :]
    n_state = CONV_WIDTH - 1
    ext = [s_ref[k * n_seq:(k + 1) * n_seq, :] for k in range(n_state)]
    ext += [u[t * n_seq:(t + 1) * n_seq] for t in range(n_t)]
    for t in range(n_t):
        conv = w0 * ext[t] + w1 * ext[t + 1] + w2 * ext[t + 2]
        y_ref[t * n_seq:(t + 1) * n_seq, :] = (b[t * n_seq:(t + 1) * n_seq] * conv).astype(BF16)
    for k in range(n_state):
        st_ref[k * n_seq:(k + 1) * n_seq, :] = ext[n_t + k]


def _conv_path_prompt(xn, conv_w, w_in, slab_weights, *, seq, d_conv, tm, tn):
    m, d = xn.shape
    nct = d_conv // tn
    n_i = m // tm
    n_steps = nct * n_i
    riders = [_slab_rider(w, n_steps, lambda j, i: j * n_i + i) for w in slab_weights]
    kern = functools.partial(_conv_prompt_kernel, tm=tm, blocks_per_seq=seq // tm, n_riders=len(riders))
    w_chunk = pl.BlockSpec((d, tn), lambda j, i: (0, j))
    return pl.pallas_call(
        kern,
        grid=(nct, n_i),
        in_specs=[
            pl.BlockSpec((tm, d), lambda j, i: (i, 0)),
            pl.BlockSpec((d, tn), lambda j, i: (0, j)),
            pl.BlockSpec((d, tn), lambda j, i: (0, nct + j)),
            pl.BlockSpec((d, tn), lambda j, i: (0, 2 * nct + j)),
            pl.BlockSpec((CONV_WIDTH, tn), lambda j, i: (0, j)),
        ] + [r[0] for r in riders],
        out_specs=[
            pl.BlockSpec((tm, tn), lambda j, i: (i, j)),
            pl.BlockSpec((1, SUBLANES, tn), lambda j, i: (i, 0, j)),
            w_chunk, w_chunk, w_chunk,
        ] + [r[1] for r in riders],
        out_shape=[
            jax.ShapeDtypeStruct((m, d_conv), BF16),
            jax.ShapeDtypeStruct((n_i, SUBLANES, d_conv), F32),
        ] + [jax.ShapeDtypeStruct((d, d_conv), BF16)] * 3 + [r[2] for r in riders],
        scratch_shapes=[pltpu.VMEM((SUBLANES, tn), F32)],
        compiler_params=_params(2),
        name="conv_path_prompt",
    )(xn, w_in, w_in, w_in, conv_w, *slab_weights)


def _conv_path_sample(x2d, g, w_h_bf, w_b_bf, w_c_bf, conv_w, state_tm, *, n_seq, n_t, d_conv, tn):
    m, d = x2d.shape
    nct = d_conv // tn
    n_state = CONV_WIDTH - 1
    kern = functools.partial(_conv_sample_kernel, n_seq=n_seq, n_t=n_t)
    return pl.pallas_call(
        kern,
        grid=(nct,),
        in_specs=[
            pl.BlockSpec((m, d), lambda j: (0, 0)),
            pl.BlockSpec((1, d), lambda j: (0, 0)),
            pl.BlockSpec((d, tn), lambda j: (0, j)),
            pl.BlockSpec((d, tn), lambda j: (0, j)),
            pl.BlockSpec((d, tn), lambda j: (0, j)),
            pl.BlockSpec((CONV_WIDTH, tn), lambda j: (0, j)),
            pl.BlockSpec((n_state * n_seq, tn), lambda j: (0, j)),
        ],
        out_specs=[
            pl.BlockSpec((m, d), lambda j: (0, 0)),
            pl.BlockSpec((m, tn), lambda j: (0, j)),
            pl.BlockSpec((n_state * n_seq, tn), lambda j: (0, j)),
        ],
        out_shape=[
            jax.ShapeDtypeStruct((m, d), BF16),
            jax.ShapeDtypeStruct((m, d_conv), BF16),
            jax.ShapeDtypeStruct((n_state * n_seq, d_conv), F32),
        ],
        compiler_params=_params(1),
        name="conv_path_sample",
    )(x2d, g, w_h_bf, w_b_bf, w_c_bf, conv_w, state_tm)


def _pool_prompt_kernel(x_ref, g_ref, wv_ref, wp_ref, ps_ref, *rest, tm, blocks_per_seq, d_group, n_riders):
    rider_src = rest[:n_riders]
    xn_ref, mixed_ref, st_ref = rest[n_riders:n_riders + 3]
    rider_dst = rest[n_riders + 3:2 * n_riders + 3]
    carry_ref, wvb_ref = rest[2 * n_riders + 3:]
    i = pl.program_id(0)
    blk = i % blocks_per_seq

    @pl.when(i == 0)
    def _():
        wvb_ref[...] = wv_ref[...].astype(BF16)

    @pl.when(blk == 0)
    def _():
        carry_ref[...] = jnp.zeros(carry_ref.shape, F32)

    xn = _rmsnorm(x_ref[...], g_ref[...]).astype(BF16)
    xn_ref[...] = xn
    pos = blk * tm + lax.broadcasted_iota(jnp.int32, (tm, 1), 0)
    n_groups = len(POOL_WINDOWS)
    order = sorted(range(n_groups), key=lambda q: (-(q // 2), q))
    v_pairs = {g // 2: _dot(xn, wvb_ref[:, g * d_group:(g + 2) * d_group]) for g in order if g % 2 == 0}
    for g in order:
        w = POOL_WINDOWS[g]
        lo, hi = g * d_group, (g + 1) * d_group
        v = v_pairs[g // 2][:, (g % 2) * d_group:(g % 2 + 1) * d_group]
        s = jnp.concatenate([carry_ref[:, lo:hi], v], axis=0)
        tail = v[tm - POOL_CARRY_ROWS:tm]
        carry_ref[:, lo:hi] = tail
        st_ref[0, :, lo:hi] = tail
        k = 1
        while k < w:
            s = s + pltpu.roll(s, k, 0)
            k *= 2
        cnt = jnp.minimum(pos + 1, w).astype(F32)
        pooled = s[POOL_CARRY_ROWS:] / cnt - v
        mixed = _dot(pooled.astype(BF16), wp_ref[g].astype(BF16)) * ps_ref[:, lo:hi]
        mixed_ref[:, lo:hi] = mixed.astype(BF16)
    _cast_riders(rider_src, rider_dst)


def _pool_sample_kernel(xn_ref, wv_ref, wp_ref, ps_ref, s_ref, mixed_ref, st_ref, *, n_seq, n_t, d_group):
    v = _dot(xn_ref[...], wv_ref[...])
    for g, w in enumerate(POOL_WINDOWS):
        lo, hi = g * d_group, (g + 1) * d_group
        ext = [s_ref[k * n_seq:(k + 1) * n_seq, lo:hi] for k in range(POOL_BUF)]
        ext += [v[t * n_seq:(t + 1) * n_seq, lo:hi] for t in range(n_t)]
        cnt = float(min(PAST_LEN + 1, w))
        wp = wp_ref[g].astype(BF16)
        for t in range(n_t):
            s = ext[POOL_BUF + t]
            for k in range(1, w):
                s = s + ext[POOL_BUF + t - k]
            pooled = s / cnt - ext[POOL_BUF + t]
            mixed = _dot(pooled.astype(BF16), wp) * ps_ref[:, lo:hi]
            mixed_ref[t * n_seq:(t + 1) * n_seq, lo:hi] = mixed.astype(BF16)
        for k in range(POOL_BUF):
            st_ref[k * n_seq:(k + 1) * n_seq, lo:hi] = ext[n_t + k]


def _pool_path_prompt(x2d, g, w_in, w_pool, pool_scale, *, n_batch, seq, d_conv, d_pool, tm):
    m, d = x2d.shape
    n_groups = len(POOL_WINDOWS)
    d_group = d_pool // n_groups
    v_off = 3 * d_conv
    riders = [_col_rider(w_in, v_off, w_in.shape[1], m // tm, lambda i: i, width=d_pool)]
    kern = functools.partial(_pool_prompt_kernel, tm=tm, blocks_per_seq=seq // tm, d_group=d_group,
                             n_riders=len(riders))
    return pl.pallas_call(
        kern,
        grid=(m // tm,),
        in_specs=[
            pl.BlockSpec((tm, d), lambda i: (i, 0)),
            pl.BlockSpec((1, d), lambda i: (0, 0)),
            pl.BlockSpec((d, d_pool), lambda i: (0, v_off // d_pool)),
            pl.BlockSpec((n_groups, d_group, d_group), lambda i: (0, 0, 0)),
            pl.BlockSpec((1, d_pool), lambda i: (0, 0)),
        ] + [r[0] for r in riders],
        out_specs=[
            pl.BlockSpec((tm, d), lambda i: (i, 0)),
            pl.BlockSpec((tm, d_pool), lambda i: (i, 0)),
            pl.BlockSpec((1, POOL_CARRY_ROWS, d_pool), lambda i: ((i * tm) // seq, 0, 0)),
        ] + [r[1] for r in riders],
        out_shape=[
            jax.ShapeDtypeStruct((m, d), BF16),
            jax.ShapeDtypeStruct((m, d_pool), BF16),
            jax.ShapeDtypeStruct((n_batch, POOL_CARRY_ROWS, d_pool), F32),
        ] + [r[2] for r in riders],
        scratch_shapes=[pltpu.VMEM((POOL_CARRY_ROWS, d_pool), F32), pltpu.VMEM((d, d_pool), BF16)],
        compiler_params=_params(1),
        name="pool_path_prompt",
    )(x2d, g, w_in, w_pool, pool_scale, w_in)


def _pool_path_sample(xn, w_tail_bf, w_pool, pool_scale, state_tm, *, n_seq, n_t, d_pool):
    m, d = xn.shape
    n_groups = len(POOL_WINDOWS)
    d_group = d_pool // n_groups
    kern = functools.partial(_pool_sample_kernel, n_seq=n_seq, n_t=n_t, d_group=d_group)
    return pl.pallas_call(
        kern,
        grid=(1,),
        in_specs=[
            pl.BlockSpec((m, d), lambda i: (0, 0)),
            pl.BlockSpec((d, d_pool), lambda i: (0, 0)),
            pl.BlockSpec((n_groups, d_group, d_group), lambda i: (0, 0, 0)),
            pl.BlockSpec((1, d_pool), lambda i: (0, 0)),
            pl.BlockSpec((POOL_BUF * n_seq, d_pool), lambda i: (0, 0)),
        ],
        out_specs=[
            pl.BlockSpec((m, d_pool), lambda i: (0, 0)),
            pl.BlockSpec((POOL_BUF * n_seq, d_pool), lambda i: (0, 0)),
        ],
        out_shape=[
            jax.ShapeDtypeStruct((m, d_pool), BF16),
            jax.ShapeDtypeStruct((POOL_BUF * n_seq, d_pool), F32),
        ],
        compiler_params=_params(1),
        name="pool_path_sample",
    )(xn, w_tail_bf, w_pool, pool_scale, state_tm)


def _gate_merge_kernel(xn_ref, yc_ref, mx_ref, wgc_ref, wgp_ref, wbc_ref, wbp_ref, *rest, n_riders):
    rider_src = rest[:n_riders]
    o_ref = rest[n_riders]
    rider_dst = rest[n_riders + 1:2 * n_riders + 1]
    xn = xn_ref[...]
    gc = _dot(xn, wgc_ref[...])
    gp = _dot(xn, wgp_ref[...])
    y_conv = _dot(yc_ref[...], wbc_ref[...])
    y_pool = _dot(mx_ref[...], wbp_ref[...])
    o_ref[...] = (jax.nn.sigmoid(gc) * y_conv + jax.nn.sigmoid(gp) * y_pool).astype(BF16)
    _cast_riders(rider_src, rider_dst)


def _gate_merge(xn, ycin, mixed, w_tail_bf, w_brc_bf, w_brp_bf, slab_weights, *, gc_off, gp_off, tm, tn):
    m, d = xn.shape
    d_conv = ycin.shape[1]
    d_pool = mixed.shape[1]
    nct = d // tn
    riders = [_slab_rider(w, (m // tm) * nct, lambda i, j: i * nct + j) for w in slab_weights]
    out = pl.pallas_call(
        functools.partial(_gate_merge_kernel, n_riders=len(riders)),
        grid=(m // tm, nct),
        in_specs=[
            pl.BlockSpec((tm, d), lambda i, j: (i, 0)),
            pl.BlockSpec((tm, d_conv), lambda i, j: (i, 0)),
            pl.BlockSpec((tm, d_pool), lambda i, j: (i, 0)),
            pl.BlockSpec((d, tn), lambda i, j: (0, gc_off // tn + j)),
            pl.BlockSpec((d, tn), lambda i, j: (0, gp_off // tn + j)),
            pl.BlockSpec((d_conv, tn), lambda i, j: (0, j)),
            pl.BlockSpec((d_pool, tn), lambda i, j: (0, j)),
        ] + [r[0] for r in riders],
        out_specs=[pl.BlockSpec((tm, tn), lambda i, j: (i, j))] + [r[1] for r in riders],
        out_shape=[jax.ShapeDtypeStruct((m, d), BF16)] + [r[2] for r in riders],
        compiler_params=_params(2),
        name="gate_merge",
    )(xn, ycin, mixed, w_tail_bf, w_tail_bf, w_brc_bf, w_brp_bf, *slab_weights)
    return out[0], out[1:]


def _out_proj_kernel(xp_ref, mp_ref, xs_ref, ms_ref, wo_ref, x1p_ref, x1s_ref, *, n_prompt):
    i = pl.program_id(0)

    @pl.when(i < n_prompt)
    def _():
        x1p_ref[...] = xp_ref[...] + _dot(mp_ref[...], wo_ref[...])

    @pl.when(i == n_prompt)
    def _():
        x1s_ref[...] = xs_ref[...] + _dot(ms_ref[...], wo_ref[...])


def _out_proj(xp, merged_p, xs, merged_s, w_o_bf, *, tm):
    m, d = xp.shape
    n_prompt = m // tm
    assert xs.shape[0] == tm
    last = n_prompt - 1
    row = lambda i: (jnp.minimum(i, last), 0)
    const = lambda i: (0, 0)
    return pl.pallas_call(
        functools.partial(_out_proj_kernel, n_prompt=n_prompt),
        grid=(n_prompt + 1,),
        in_specs=[
            pl.BlockSpec((tm, d), row),
            pl.BlockSpec((tm, d), row),
            pl.BlockSpec((tm, d), const),
            pl.BlockSpec((tm, d), const),
            pl.BlockSpec((d, d), const, pipeline_mode=pl.Buffered(1)),
        ],
        out_specs=[pl.BlockSpec((tm, d), row), pl.BlockSpec((tm, d), const)],
        out_shape=[jax.ShapeDtypeStruct((m, d), F32), jax.ShapeDtypeStruct((tm, d), F32)],
        compiler_params=_params(1),
        name="out_proj_residual",
    )(xp, merged_p, xs, merged_s, w_o_bf)


def _ffn_kernel(xp_ref, xs_ref, wg_ref, wu_ref, wd_ref, gf_ref, gl_ref, op_ref, os_ref, hn_ref, *, final_norm):
    f = pl.program_id(1)
    tm = xp_ref.shape[0]

    @pl.when(f == 0)
    def _():
        xp = xp_ref[...]
        xs = xs_ref[...]
        hn_ref[0:tm, :] = _rmsnorm(xp, gf_ref[...]).astype(BF16)
        hn_ref[tm:, :] = _rmsnorm(xs, gf_ref[...]).astype(BF16)
        op_ref[...] = xp
        os_ref[...] = xs

    hn = hn_ref[...]
    half = wg_ref.shape[1] // 2
    ff = []
    for k in range(2):
        cols = slice(k * half, (k + 1) * half)
        gate = _dot(hn, wg_ref[:, cols])
        up = _dot(hn, wu_ref[:, cols])
        ff.append(((gate * jax.nn.sigmoid(gate)) * up).astype(BF16))
    down = _dot(jnp.concatenate(ff, axis=1), wd_ref[...])
    op_ref[...] += down[0:tm]
    os_ref[...] += down[tm:]

    if final_norm:
        @pl.when(f == pl.num_programs(1) - 1)
        def _():
            op_ref[...] = _rmsnorm(op_ref[...], gl_ref[...])
            os_ref[...] = _rmsnorm(os_ref[...], gl_ref[...])


def _ffn(x1p, x1s, w_gate_bf, w_up_bf, w_down_bf, g_ffn, g_final, *, final_norm, tm, tf):
    m, d = x1p.shape
    d_ff = w_gate_bf.shape[1]
    n_blocks = m // tm
    rs = x1s.shape[0] // n_blocks
    assert rs * n_blocks == x1s.shape[0] and rs % HEAD_ROWS == 0
    kern = functools.partial(_ffn_kernel, final_norm=final_norm)
    return pl.pallas_call(
        kern,
        grid=(n_blocks, d_ff // tf),
        in_specs=[
            pl.BlockSpec((tm, d), lambda i, f: (i, 0)),
            pl.BlockSpec((rs, d), lambda i, f: (i, 0)),
            pl.BlockSpec((d, tf), lambda i, f: (0, f)),
            pl.BlockSpec((d, tf), lambda i, f: (0, f)),
            pl.BlockSpec((tf, d), lambda i, f: (f, 0)),
            pl.BlockSpec((1, d), lambda i, f: (0, 0)),
            pl.BlockSpec((1, d), lambda i, f: (0, 0)),
        ],
        out_specs=[pl.BlockSpec((tm, d), lambda i, f: (i, 0)), pl.BlockSpec((rs, d), lambda i, f: (i, 0))],
        out_shape=[jax.ShapeDtypeStruct((m, d), F32), jax.ShapeDtypeStruct(x1s.shape, F32)],
        scratch_shapes=[pltpu.VMEM((tm + rs, d), BF16)],
        compiler_params=_params(2),
        name="swiglu_ffn",
    )(x1p, x1s, w_gate_bf, w_up_bf, w_down_bf, g_ffn, g_final)


def _tiles(d_conv, d_ff):
    tm = 1024
    tm_pool = 512
    tn_conv = 512 if d_conv % 512 == 0 else 256
    tn_merge = 512
    tm_out = 512
    tf = 512 if d_ff % 512 == 0 else 256
    return tm, tm_pool, tm_out, tn_conv, tn_merge, tf


def kernel(x_prompt, x_sample, state_conv, state_pool, norm_mix, w_in, conv_w, w_pool, pool_scale, w_br_conv,
           w_br_pool, w_o, norm_ffn, w_gate, w_up, w_down, norm_final):
    n_batch, seq, d = x_prompt.shape
    n_seq, n_t, _ = x_sample.shape
    depth = w_in.shape[0]
    d_conv = conv_w.shape[2]
    d_pool = pool_scale.shape[1]
    d_ff = w_gate.shape[2]
    gc_off = d_pool
    gp_off = gc_off + d
    tm, tm_pool, tm_out, tn_conv, tn_merge, tf = _tiles(d_conv, d_ff)
    assert seq % tm == 0 and seq % tm_pool == 0 and tm_pool >= POOL_CARRY_ROWS and n_t >= CONV_WIDTH - 1
    assert (n_batch * seq) % tm_out == 0 and n_t * n_seq == tm_out
    assert n_seq % 16 == 0 and gc_off % tn_merge == 0 and gp_off % tn_merge == 0

    yp = x_prompt.reshape(n_batch * seq, d)
    ys = jnp.transpose(x_sample, (1, 0, 2)).reshape(n_t * n_seq, d)
    g_final = norm_final.reshape(1, d)
    conv_p, pool_p, conv_s, pool_s = [], [], [], []
    for l in range(depth):
        g_mix = norm_mix[l].reshape(1, d)
        g_ffn = norm_ffn[l].reshape(1, d)
        ps = pool_scale[l].reshape(1, d_pool)
        last = l == depth - 1

        xn, mixed, st_p, w_tail_bf = _pool_path_prompt(yp, g_mix, w_in[l], w_pool[l], ps, n_batch=n_batch, seq=seq,
                                                       d_conv=d_conv, d_pool=d_pool, tm=tm_pool)
        ycin, st_c, w_h_bf, w_b_bf, w_c_bf, w_brc_bf, w_brp_bf = _conv_path_prompt(
            xn, conv_w[l], w_in[l], (w_br_conv[l], w_br_pool[l]), seq=seq, d_conv=d_conv, tm=tm, tn=tn_conv)
        merged, (w_o_bf, w_down_bf, w_gate_bf, w_up_bf) = _gate_merge(
            xn, ycin, mixed, w_tail_bf, w_brc_bf, w_brp_bf, (w_o[l], w_down[l], w_gate[l], w_up[l]),
            gc_off=gc_off, gp_off=gp_off, tm=tm, tn=tn_merge)
        bps = seq // tm
        conv_p.append(st_c[bps - 1::bps, SUBLANES - (CONV_WIDTH - 1):, :])
        pool_p.append(st_p[:, POOL_CARRY_ROWS - POOL_BUF:, :])

        sc_tm = jnp.transpose(state_conv[l], (1, 0, 2)).reshape((CONV_WIDTH - 1) * n_seq, d_conv)
        sp_tm = jnp.transpose(state_pool[l], (1, 0, 2)).reshape(POOL_BUF * n_seq, d_pool)
        xn_s, ycin_s, st_c = _conv_path_sample(ys, g_mix, w_h_bf, w_b_bf, w_c_bf, conv_w[l], sc_tm, n_seq=n_seq,
                                               n_t=n_t, d_conv=d_conv, tn=tn_conv)
        mixed_s, st_p = _pool_path_sample(xn_s, w_tail_bf, w_pool[l], ps, sp_tm, n_seq=n_seq, n_t=n_t,
                                          d_pool=d_pool)
        merged_s, _ = _gate_merge(xn_s, ycin_s, mixed_s, w_tail_bf, w_brc_bf, w_brp_bf, (),
                                  gc_off=gc_off, gp_off=gp_off, tm=n_t * n_seq, tn=tn_merge)
        conv_s.append(jnp.transpose(st_c.reshape(CONV_WIDTH - 1, n_seq, d_conv), (1, 0, 2)))
        pool_s.append(jnp.transpose(st_p.reshape(POOL_BUF, n_seq, d_pool), (1, 0, 2)))

        x1p, x1s = _out_proj(yp, merged, ys, merged_s, w_o_bf, tm=tm_out)
        yp, ys = _ffn(x1p, x1s, w_gate_bf, w_up_bf, w_down_bf, g_ffn, g_final, final_norm=last, tm=tm, tf=tf)

    y_prompt = yp.reshape(n_batch, seq, d)
    y_sample = jnp.transpose(ys.reshape(n_t, n_seq, d), (1, 0, 2))
    return (y_prompt, y_sample, jnp.stack(conv_p, axis=0), jnp.stack(pool_p, axis=0),
            jnp.stack(conv_s, axis=0), jnp.stack(pool_s, axis=0))
```

```python
import functools

import jax
import jax.numpy as jnp
from jax import lax
from jax.experimental import pallas as pl
from jax.experimental.pallas import tpu as pltpu

EPS = 1e-6
CONV_WIDTH = 3
POOL_WINDOWS = (2, 4, 8, 16)
POOL_BUF = max(POOL_WINDOWS) - 1
PAST_LEN = 16384

SUBLANES = 8
HEAD_ROWS = 16
POOL_CARRY_ROWS = 16
VMEM_LIMIT_BYTES = 56 * 1024 * 1024
HIGH_VMEM_LIMIT_BYTES = 62 * 1024 * 1024

BF16 = jnp.bfloat16
F32 = jnp.float32


def _dot(a, b):
    return jnp.dot(a, b, preferred_element_type=F32)


def _rmsnorm(x, g):
    r = lax.rsqrt(jnp.mean(x * x, axis=-1, keepdims=True) + EPS)
    return (x * r) * g


def _params(n_axes, vmem_limit_bytes=VMEM_LIMIT_BYTES):
    return pltpu.CompilerParams(dimension_semantics=("arbitrary",) * n_axes,
                                vmem_limit_bytes=vmem_limit_bytes)


def _cast_riders(src_refs, dst_refs):
    for src, dst in zip(src_refs, dst_refs):
        dst[...] = src[...].astype(BF16)


def _slab_rider(w, n_steps, step_of):
    rows, cols = w.shape
    slab = rows // n_steps
    assert slab * n_steps == rows and slab % HEAD_ROWS == 0
    spec = pl.BlockSpec((slab, cols), lambda *ids: (step_of(*ids), 0))
    return spec, spec, jax.ShapeDtypeStruct((rows, cols), BF16)


def _col_rider(w, col0, col1, n_steps, step_of, *, width):
    rows = w.shape[0]
    n_col = (col1 - col0) // width
    assert n_col * width == col1 - col0 and col0 % width == 0 and n_col <= n_steps
    n_row = 1
    while n_row * 2 * n_col <= n_steps and rows % (n_row * 2) == 0:
        n_row *= 2
    rb = rows // n_row
    assert rb % HEAD_ROWS == 0
    n_blk = n_row * n_col

    def blk(*ids):
        return jnp.minimum(step_of(*ids), n_blk - 1)

    in_spec = pl.BlockSpec((rb, width), lambda *ids: (blk(*ids) // n_col, col0 // width + blk(*ids) % n_col))
    out_spec = pl.BlockSpec((rb, width), lambda *ids: (blk(*ids) // n_col, blk(*ids) % n_col))
    return in_spec, out_spec, jax.ShapeDtypeStruct((rows, col1 - col0), BF16)


def _conv_prompt_kernel(xn_ref, wh_ref, wb_ref, wc_ref, cw_ref, *rest, tm, blocks_per_seq, n_riders):
    rider_src = rest[:n_riders]
    y_ref, st_ref, whb_ref, wbb_ref, wcb_ref = rest[n_riders:n_riders + 5]
    rider_dst = rest[n_riders + 5:2 * n_riders + 5]
    carry_ref = rest[2 * n_riders + 5]
    i = pl.program_id(1)

    @pl.when(i == 0)
    def _():
        whb_ref[...] = wh_ref[...].astype(BF16)
        wbb_ref[...] = wb_ref[...].astype(BF16)
        wcb_ref[...] = wc_ref[...].astype(BF16)

    @pl.when(i % blocks_per_seq == 0)
    def _():
        carry_ref[...] = jnp.zeros(carry_ref.shape, F32)

    xn = xn_ref[...]
    c = _dot(xn, wcb_ref[...])
    h = _dot(xn, whb_ref[...])
    u = c * h
    b = _dot(xn, wbb_ref[...])
    w0 = cw_ref[0:1, :]
    w1 = cw_ref[1:2, :]
    w2 = cw_ref[2:3, :]
    conv = w0 * pltpu.roll(u, 2, 0) + w1 * pltpu.roll(u, 1, 0) + w2 * u
    y_ref[...] = (b * conv).astype(BF16)
    head = jnp.concatenate([carry_ref[...], u[0:HEAD_ROWS]], axis=0)
    conv_head = w0 * pltpu.roll(head, 2, 0) + w1 * pltpu.roll(head, 1, 0) + w2 * head
    y_ref[0:HEAD_ROWS, :] = (b[0:HEAD_ROWS] * conv_head[SUBLANES:]).astype(BF16)
    tail = u[tm - SUBLANES:tm]
    carry_ref[...] = tail
    st_ref[0] = tail
    _cast_riders(rider_src, rider_dst)


def _conv_sample_kernel(x_ref, g_ref, wh_ref, wb_ref, wc_ref, cw_ref, s_ref, xn_ref, y_ref, st_ref,
                        *, n_seq, n_t):
    j = pl.program_id(0)

    @pl.when(j == 0)
    def _():
        xn_ref[...] = _rmsnorm(x_ref[...], g_ref[...]).astype(BF16)

    xn = xn_ref[...]
    c = _dot(xn, wc_ref[...])
    h = _dot(xn, wh_ref[...])
    u = c * h
    b = _dot(xn, wb_ref[...])
    w0 = cw_ref[0:1, :]
    w1 = cw_ref[1:2, :]
    w2 = cw_ref[2:3, :]
    n_state = CONV_WIDTH - 1
    ext = [s_ref[k * n_seq:(k + 1) * n_seq, :] for k in range(n_state)]
    ext += [u[t * n_seq:(t + 1) * n_seq] for t in range(n_t)]
    for t in range(n_t):
        conv = w0 * ext[t] + w1 * ext[t + 1] + w2 * ext[t + 2]
        y_ref[t * n_seq:(t + 1) * n_seq, :] = (b[t * n_seq:(t + 1) * n_seq] * conv).astype(BF16)
    for k in range(n_state):
        st_ref[k * n_seq:(k + 1) * n_seq, :] = ext[n_t + k]


def _conv_path_prompt(xn, conv_w, w_in, slab_weights, *, seq, d_conv, tm, tn):
    m, d = xn.shape
    nct = d_conv // tn
    n_i = m // tm
    n_steps = nct * n_i
    step_of = lambda j, i: j * n_i + i
    riders = [_col_rider(w_in, 3 * d_conv, w_in.shape[1], n_steps, step_of, width=tn * 2)]
    riders += [_slab_rider(w, n_steps, step_of) for w in slab_weights]
    kern = functools.partial(_conv_prompt_kernel, tm=tm, blocks_per_seq=seq // tm, n_riders=len(riders))
    return pl.pallas_call(
        kern,
        grid=(nct, n_i),
        in_specs=[
            pl.BlockSpec((tm, d), lambda j, i: (i, 0)),
            pl.BlockSpec((d, tn), lambda j, i: (0, j)),
            pl.BlockSpec((d, tn), lambda j, i: (0, nct + j)),
            pl.BlockSpec((d, tn), lambda j, i: (0, 2 * nct + j)),
            pl.BlockSpec((CONV_WIDTH, tn), lambda j, i: (0, j)),
        ] + [r[0] for r in riders],
        out_specs=[
            pl.BlockSpec((tm, tn), lambda j, i: (i, j)),
            pl.BlockSpec((1, SUBLANES, tn), lambda j, i: (i, 0, j)),
        ] + [pl.BlockSpec((d, tn), lambda j, i: (0, j))] * 3 + [r[1] for r in riders],
        out_shape=[
            jax.ShapeDtypeStruct((m, d_conv), BF16),
            jax.ShapeDtypeStruct((n_i, SUBLANES, d_conv), F32),
        ] + [jax.ShapeDtypeStruct((d, d_conv), BF16)] * 3 + [r[2] for r in riders],
        scratch_shapes=[pltpu.VMEM((SUBLANES, tn), F32)],
        compiler_params=_params(2, HIGH_VMEM_LIMIT_BYTES),
        name="conv_path_prompt",
    )(xn, w_in, w_in, w_in, conv_w, w_in, *slab_weights)


def _conv_path_sample(x2d, g, w_h_bf, w_b_bf, w_c_bf, conv_w, state_tm, *, n_seq, n_t, d_conv, tn):
    m, d = x2d.shape
    nct = d_conv // tn
    n_state = CONV_WIDTH - 1
    kern = functools.partial(_conv_sample_kernel, n_seq=n_seq, n_t=n_t)
    return pl.pallas_call(
        kern,
        grid=(nct,),
        in_specs=[
            pl.BlockSpec((m, d), lambda j: (0, 0)),
            pl.BlockSpec((1, d), lambda j: (0, 0)),
            pl.BlockSpec((d, tn), lambda j: (0, j)),
            pl.BlockSpec((d, tn), lambda j: (0, j)),
            pl.BlockSpec((d, tn), lambda j: (0, j)),
            pl.BlockSpec((CONV_WIDTH, tn), lambda j: (0, j)),
            pl.BlockSpec((n_state * n_seq, tn), lambda j: (0, j)),
        ],
        out_specs=[
            pl.BlockSpec((m, d), lambda j: (0, 0)),
            pl.BlockSpec((m, tn), lambda j: (0, j)),
            pl.BlockSpec((n_state * n_seq, tn), lambda j: (0, j)),
        ],
        out_shape=[
            jax.ShapeDtypeStruct((m, d), BF16),
            jax.ShapeDtypeStruct((m, d_conv), BF16),
            jax.ShapeDtypeStruct((n_state * n_seq, d_conv), F32),
        ],
        compiler_params=_params(1),
        name="conv_path_sample",
    )(x2d, g, w_h_bf, w_b_bf, w_c_bf, conv_w, state_tm)


def _pool_prompt_kernel(x_ref, g_ref, wv_ref, wp_ref, ps_ref, xn_ref, mixed_ref, st_ref, carry_ref, wvb_ref,
                        *, tm, blocks_per_seq, d_group):
    i = pl.program_id(0)
    blk = i % blocks_per_seq

    @pl.when(i == 0)
    def _():
        wvb_ref[...] = wv_ref[...].astype(BF16)

    @pl.when(blk == 0)
    def _():
        carry_ref[...] = jnp.zeros(carry_ref.shape, F32)

    xn = _rmsnorm(x_ref[...], g_ref[...]).astype(BF16)
    xn_ref[...] = xn
    pos = blk * tm + lax.broadcasted_iota(jnp.int32, (tm, 1), 0)
    n_groups = len(POOL_WINDOWS)
    order = sorted(range(n_groups), key=lambda q: (-(q // 2), q))
    v_pairs = {g // 2: _dot(xn, wvb_ref[:, g * d_group:(g + 2) * d_group]) for g in order if g % 2 == 0}
    for g in order:
        w = POOL_WINDOWS[g]
        lo, hi = g * d_group, (g + 1) * d_group
        v = v_pairs[g // 2][:, (g % 2) * d_group:(g % 2 + 1) * d_group]
        s = jnp.concatenate([carry_ref[:, lo:hi], v], axis=0)
        tail = v[tm - POOL_CARRY_ROWS:tm]
        carry_ref[:, lo:hi] = tail
        st_ref[0, :, lo:hi] = tail
        k = 1
        while k < w:
            s = s + pltpu.roll(s, k, 0)
            k *= 2
        cnt = jnp.minimum(pos + 1, w).astype(F32)
        pooled = s[POOL_CARRY_ROWS:] / cnt - v
        mixed = _dot(pooled.astype(BF16), wp_ref[g].astype(BF16)) * ps_ref[:, lo:hi]
        mixed_ref[:, lo:hi] = mixed.astype(BF16)


def _pool_sample_kernel(xn_ref, wv_ref, wp_ref, ps_ref, s_ref, mixed_ref, st_ref, *, n_seq, n_t, d_group):
    v = _dot(xn_ref[...], wv_ref[...])
    for g, w in enumerate(POOL_WINDOWS):
        lo, hi = g * d_group, (g + 1) * d_group
        ext = [s_ref[k * n_seq:(k + 1) * n_seq, lo:hi] for k in range(POOL_BUF)]
        ext += [v[t * n_seq:(t + 1) * n_seq, lo:hi] for t in range(n_t)]
        cnt = float(min(PAST_LEN + 1, w))
        wp = wp_ref[g].astype(BF16)
        for t in range(n_t):
            s = ext[POOL_BUF + t]
            for k in range(1, w):
                s = s + ext[POOL_BUF + t - k]
            pooled = s / cnt - ext[POOL_BUF + t]
            mixed = _dot(pooled.astype(BF16), wp) * ps_ref[:, lo:hi]
            mixed_ref[t * n_seq:(t + 1) * n_seq, lo:hi] = mixed.astype(BF16)
        for k in range(POOL_BUF):
            st_ref[k * n_seq:(k + 1) * n_seq, lo:hi] = ext[n_t + k]


def _pool_path_prompt(x2d, g, w_in, w_pool, pool_scale, *, n_batch, seq, d_conv, d_pool, tm):
    m, d = x2d.shape
    n_groups = len(POOL_WINDOWS)
    d_group = d_pool // n_groups
    v_off = 3 * d_conv
    kern = functools.partial(_pool_prompt_kernel, tm=tm, blocks_per_seq=seq // tm, d_group=d_group)
    return pl.pallas_call(
        kern,
        grid=(m // tm,),
        in_specs=[
            pl.BlockSpec((tm, d), lambda i: (i, 0)),
            pl.BlockSpec((1, d), lambda i: (0, 0)),
            pl.BlockSpec((d, d_pool), lambda i: (0, v_off // d_pool)),
            pl.BlockSpec((n_groups, d_group, d_group), lambda i: (0, 0, 0)),
            pl.BlockSpec((1, d_pool), lambda i: (0, 0)),
        ],
        out_specs=[
            pl.BlockSpec((tm, d), lambda i: (i, 0)),
            pl.BlockSpec((tm, d_pool), lambda i: (i, 0)),
            pl.BlockSpec((1, POOL_CARRY_ROWS, d_pool), lambda i: ((i * tm) // seq, 0, 0)),
        ],
        out_shape=[
            jax.ShapeDtypeStruct((m, d), BF16),
            jax.ShapeDtypeStruct((m, d_pool), BF16),
            jax.ShapeDtypeStruct((n_batch, POOL_CARRY_ROWS, d_pool), F32),
        ],
        scratch_shapes=[pltpu.VMEM((POOL_CARRY_ROWS, d_pool), F32), pltpu.VMEM((d, d_pool), BF16)],
        compiler_params=_params(1),
        name="pool_path_prompt",
    )(x2d, g, w_in, w_pool, pool_scale)


def _pool_path_sample(xn, w_tail_bf, w_pool, pool_scale, state_tm, *, n_seq, n_t, d_pool):
    m, d = xn.shape
    n_groups = len(POOL_WINDOWS)
    d_group = d_pool // n_groups
    kern = functools.partial(_pool_sample_kernel, n_seq=n_seq, n_t=n_t, d_group=d_group)
    return pl.pallas_call(
        kern,
        grid=(1,),
        in_specs=[
            pl.BlockSpec((m, d), lambda i: (0, 0)),
            pl.BlockSpec((d, d_pool), lambda i: (0, 0)),
            pl.BlockSpec((n_groups, d_group, d_group), lambda i: (0, 0, 0)),
            pl.BlockSpec((1, d_pool), lambda i: (0, 0)),
            pl.BlockSpec((POOL_BUF * n_seq, d_pool), lambda i: (0, 0)),
        ],
        out_specs=[
            pl.BlockSpec((m, d_pool), lambda i: (0, 0)),
            pl.BlockSpec((POOL_BUF * n_seq, d_pool), lambda i: (0, 0)),
        ],
        out_shape=[
            jax.ShapeDtypeStruct((m, d_pool), BF16),
            jax.ShapeDtypeStruct((POOL_BUF * n_seq, d_pool), F32),
        ],
        compiler_params=_params(1),
        name="pool_path_sample",
    )(xn, w_tail_bf, w_pool, pool_scale, state_tm)


def _gate_merge_kernel(xn_ref, yc_ref, mx_ref, wgc_ref, wgp_ref, wbc_ref, wbp_ref, *rest, n_riders):
    rider_src = rest[:n_riders]
    o_ref = rest[n_riders]
    rider_dst = rest[n_riders + 1:2 * n_riders + 1]
    xn = xn_ref[...]
    gc = _dot(xn, wgc_ref[...])
    gp = _dot(xn, wgp_ref[...])
    y_conv = _dot(yc_ref[...], wbc_ref[...])
    y_pool = _dot(mx_ref[...], wbp_ref[...])
    o_ref[...] = (jax.nn.sigmoid(gc) * y_conv + jax.nn.sigmoid(gp) * y_pool).astype(BF16)
    _cast_riders(rider_src, rider_dst)


def _gate_merge(xn, ycin, mixed, w_tail_bf, w_brc_bf, w_brp_bf, slab_weights, *, gc_off, gp_off, tm, tn):
    m, d = xn.shape
    d_conv = ycin.shape[1]
    d_pool = mixed.shape[1]
    nct = d // tn
    riders = [_slab_rider(w, (m // tm) * nct, lambda i, j: i * nct + j) for w in slab_weights]
    out = pl.pallas_call(
        functools.partial(_gate_merge_kernel, n_riders=len(riders)),
        grid=(m // tm, nct),
        in_specs=[
            pl.BlockSpec((tm, d), lambda i, j: (i, 0)),
            pl.BlockSpec((tm, d_conv), lambda i, j: (i, 0)),
            pl.BlockSpec((tm, d_pool), lambda i, j: (i, 0)),
            pl.BlockSpec((d, tn), lambda i, j: (0, gc_off // tn + j)),
            pl.BlockSpec((d, tn), lambda i, j: (0, gp_off // tn + j)),
            pl.BlockSpec((d_conv, tn), lambda i, j: (0, j)),
            pl.BlockSpec((d_pool, tn), lambda i, j: (0, j)),
        ] + [r[0] for r in riders],
        out_specs=[pl.BlockSpec((tm, tn), lambda i, j: (i, j))] + [r[1] for r in riders],
        out_shape=[jax.ShapeDtypeStruct((m, d), BF16)] + [r[2] for r in riders],
        compiler_params=_params(2),
        name="gate_merge",
    )(xn, ycin, mixed, w_tail_bf, w_tail_bf, w_brc_bf, w_brp_bf, *slab_weights)
    return out[0], out[1:]


def _out_proj_kernel(xp_ref, mp_ref, xs_ref, ms_ref, wo_ref, x1p_ref, x1s_ref, *, n_prompt):
    i = pl.program_id(0)

    @pl.when(i < n_prompt)
    def _():
        x1p_ref[...] = xp_ref[...] + _dot(mp_ref[...], wo_ref[...])

    @pl.when(i == n_prompt)
    def _():
        x1s_ref[...] = xs_ref[...] + _dot(ms_ref[...], wo_ref[...])


def _out_proj(xp, merged_p, xs, merged_s, w_o_bf, *, tm):
    m, d = xp.shape
    n_prompt = m // tm
    assert xs.shape[0] == tm
    last = n_prompt - 1
    row = lambda i: (jnp.minimum(i, last), 0)
    const = lambda i: (0, 0)
    return pl.pallas_call(
        functools.partial(_out_proj_kernel, n_prompt=n_prompt),
        grid=(n_prompt + 1,),
        in_specs=[
            pl.BlockSpec((tm, d), row),
            pl.BlockSpec((tm, d), row),
            pl.BlockSpec((tm, d), const),
            pl.BlockSpec((tm, d), const),
            pl.BlockSpec((d, d), const, pipeline_mode=pl.Buffered(1)),
        ],
        out_specs=[pl.BlockSpec((tm, d), row), pl.BlockSpec((tm, d), const)],
        out_shape=[jax.ShapeDtypeStruct((m, d), F32), jax.ShapeDtypeStruct((tm, d), F32)],
        compiler_params=_params(1),
        name="out_proj_residual",
    )(xp, merged_p, xs, merged_s, w_o_bf)


def _ffn_kernel(xp_ref, xs_ref, wg_ref, wu_ref, wd_ref, gf_ref, gl_ref, op_ref, os_ref, hn_ref, *, final_norm):
    f = pl.program_id(1)
    n_f = pl.num_programs(1)
    tm = xp_ref.shape[0]

    def down_proj():
        hn = hn_ref[...]
        half = wg_ref.shape[1] // 2
        ff = []
        for k in range(2):
            cols = slice(k * half, (k + 1) * half)
            gate = _dot(hn, wg_ref[:, cols])
            up = _dot(hn, wu_ref[:, cols])
            ff.append(((gate * jax.nn.sigmoid(gate)) * up).astype(BF16))
        return _dot(jnp.concatenate(ff, axis=1), wd_ref[...])

    @pl.when(f == 0)
    def _():
        hn_ref[0:tm, :] = _rmsnorm(xp_ref[...], gf_ref[...]).astype(BF16)
        hn_ref[tm:, :] = _rmsnorm(xs_ref[...], gf_ref[...]).astype(BF16)
        down = down_proj()
        op_ref[...] = down[0:tm]
        os_ref[...] = down[tm:]

    @pl.when(jnp.logical_and(f > 0, f < n_f - 1))
    def _():
        down = down_proj()
        op_ref[...] += down[0:tm]
        os_ref[...] += down[tm:]

    @pl.when(f == n_f - 1)
    def _():
        down = down_proj()
        yp = op_ref[...] + down[0:tm] + xp_ref[...]
        ys = os_ref[...] + down[tm:] + xs_ref[...]
        op_ref[...] = _rmsnorm(yp, gl_ref[...]) if final_norm else yp
        os_ref[...] = _rmsnorm(ys, gl_ref[...]) if final_norm else ys


def _ffn(x1p, x1s, w_gate_bf, w_up_bf, w_down_bf, g_ffn, g_final, *, final_norm, tm, tf):
    m, d = x1p.shape
    d_ff = w_gate_bf.shape[1]
    n_blocks = m // tm
    rs = x1s.shape[0] // n_blocks
    assert rs * n_blocks == x1s.shape[0] and rs % HEAD_ROWS == 0
    assert d_ff // tf >= 2
    kern = functools.partial(_ffn_kernel, final_norm=final_norm)
    return pl.pallas_call(
        kern,
        grid=(n_blocks, d_ff // tf),
        in_specs=[
            pl.BlockSpec((tm, d), lambda i, f: (i, 0)),
            pl.BlockSpec((rs, d), lambda i, f: (i, 0)),
            pl.BlockSpec((d, tf), lambda i, f: (0, f)),
            pl.BlockSpec((d, tf), lambda i, f: (0, f)),
            pl.BlockSpec((tf, d), lambda i, f: (f, 0)),
            pl.BlockSpec((1, d), lambda i, f: (0, 0)),
            pl.BlockSpec((1, d), lambda i, f: (0, 0)),
        ],
        out_specs=[pl.BlockSpec((tm, d), lambda i, f: (i, 0)), pl.BlockSpec((rs, d), lambda i, f: (i, 0))],
        out_shape=[jax.ShapeDtypeStruct((m, d), F32), jax.ShapeDtypeStruct(x1s.shape, F32)],
        scratch_shapes=[pltpu.VMEM((tm + rs, d), BF16)],
        compiler_params=_params(2, HIGH_VMEM_LIMIT_BYTES),
        name="swiglu_ffn",
    )(x1p, x1s, w_gate_bf, w_up_bf, w_down_bf, g_ffn, g_final)


def _tiles(d_conv, d_ff):
    tm = 1024
    tm_pool = 1024
    tn_conv = 512 if d_conv % 512 == 0 else 256
    tn_merge = 512
    tf = 512 if d_ff % 512 == 0 else 256
    return tm, tm_pool, tn_conv, tn_merge, tf


def kernel(x_prompt, x_sample, state_conv, state_pool, norm_mix, w_in, conv_w, w_pool, pool_scale, w_br_conv,
           w_br_pool, w_o, norm_ffn, w_gate, w_up, w_down, norm_final):
    n_batch, seq, d = x_prompt.shape
    n_seq, n_t, _ = x_sample.shape
    depth = w_in.shape[0]
    d_conv = conv_w.shape[2]
    d_pool = pool_scale.shape[1]
    d_ff = w_gate.shape[2]
    gc_off = d_pool
    gp_off = gc_off + d
    tm, tm_pool, tn_conv, tn_merge, tf = _tiles(d_conv, d_ff)
    tm_out = n_t * n_seq
    assert seq % tm == 0 and seq % tm_pool == 0 and tm_pool >= POOL_CARRY_ROWS and n_t >= CONV_WIDTH - 1
    assert (n_batch * seq) % tm_out == 0 and tm_out % HEAD_ROWS == 0
    assert n_seq % 16 == 0 and gc_off % tn_merge == 0 and gp_off % tn_merge == 0

    yp = x_prompt.reshape(n_batch * seq, d)
    ys = jnp.transpose(x_sample, (1, 0, 2)).reshape(n_t * n_seq, d)
    g_final = norm_final.reshape(1, d)
    conv_p, pool_p, conv_s, pool_s = [], [], [], []
    for l in range(depth):
        g_mix = norm_mix[l].reshape(1, d)
        g_ffn = norm_ffn[l].reshape(1, d)
        ps = pool_scale[l].reshape(1, d_pool)
        last = l == depth - 1

        xn, mixed, st_p = _pool_path_prompt(yp, g_mix, w_in[l], w_pool[l], ps, n_batch=n_batch, seq=seq,
                                            d_conv=d_conv, d_pool=d_pool, tm=tm_pool)
        ycin, st_c, w_h_bf, w_b_bf, w_c_bf, w_tail_bf, w_brc_bf, w_brp_bf = _conv_path_prompt(
            xn, conv_w[l], w_in[l], (w_br_conv[l], w_br_pool[l]), seq=seq, d_conv=d_conv, tm=tm, tn=tn_conv)
        merged, (w_o_bf, w_down_bf, w_gate_bf, w_up_bf) = _gate_merge(
            xn, ycin, mixed, w_tail_bf, w_brc_bf, w_brp_bf, (w_o[l], w_down[l], w_gate[l], w_up[l]),
            gc_off=gc_off, gp_off=gp_off, tm=tm, tn=tn_merge)
        bps = seq // tm
        conv_p.append(st_c[bps - 1::bps, SUBLANES - (CONV_WIDTH - 1):, :])
        pool_p.append(st_p[:, POOL_CARRY_ROWS - POOL_BUF:, :])

        sc_tm = jnp.transpose(state_conv[l], (1, 0, 2)).reshape((CONV_WIDTH - 1) * n_seq, d_conv)
        sp_tm = jnp.transpose(state_pool[l], (1, 0, 2)).reshape(POOL_BUF * n_seq, d_pool)
        xn_s, ycin_s, st_c = _conv_path_sample(ys, g_mix, w_h_bf, w_b_bf, w_c_bf, conv_w[l], sc_tm, n_seq=n_seq,
                                               n_t=n_t, d_conv=d_conv, tn=tn_conv)
        mixed_s, st_p = _pool_path_sample(xn_s, w_tail_bf, w_pool[l], ps, sp_tm, n_seq=n_seq, n_t=n_t,
                                          d_pool=d_pool)
        merged_s, _ = _gate_merge(xn_s, ycin_s, mixed_s, w_tail_bf, w_brc_bf, w_brp_bf, (),
                                  gc_off=gc_off, gp_off=gp_off, tm=n_t * n_seq, tn=tn_merge)
        conv_s.append(jnp.transpose(st_c.reshape(CONV_WIDTH - 1, n_seq, d_conv), (1, 0, 2)))
        pool_s.append(jnp.transpose(st_p.reshape(POOL_BUF, n_seq, d_pool), (1, 0, 2)))

        x1p, x1s = _out_proj(yp, merged, ys, merged_s, w_o_bf, tm=tm_out)
        yp, ys = _ffn(x1p, x1s, w_gate_bf, w_up_bf, w_down_bf, g_ffn, g_final, final_norm=last, tm=tm, tf=tf)

    y_prompt = yp.reshape(n_batch, seq, d)
    y_sample = jnp.transpose(ys.reshape(n_t, n_seq, d), (1, 0, 2))
    return (y_prompt, y_sample, jnp.stack(conv_p, axis=0), jnp.stack(pool_p, axis=0),
            jnp.stack(conv_s, axis=0), jnp.stack(pool_s, axis=0))
```

```python
import functools

import jax
import jax.numpy as jnp
from jax import lax
from jax.experimental import pallas as pl
from jax.experimental.pallas import tpu as pltpu

EPS = 1e-6
CONV_WIDTH = 3
POOL_WINDOWS = (2, 4, 8, 16)
POOL_BUF = max(POOL_WINDOWS) - 1
PAST_LEN = 16384

SUBLANES = 8
HEAD_ROWS = 16
POOL_CARRY_ROWS = 16
POOL_PAIR = 2
VMEM_LIMIT_BYTES = 56 * 1024 * 1024
HIGH_VMEM_LIMIT_BYTES = 62 * 1024 * 1024

BF16 = jnp.bfloat16
F32 = jnp.float32


def _dot(a, b):
    return jnp.dot(a, b, preferred_element_type=F32)


def _rmsnorm(x, g):
    r = lax.rsqrt(jnp.mean(x * x, axis=-1, keepdims=True) + EPS)
    return (x * r) * g


def _params(n_axes, vmem_limit_bytes=VMEM_LIMIT_BYTES):
    return pltpu.CompilerParams(dimension_semantics=("arbitrary",) * n_axes,
                                vmem_limit_bytes=vmem_limit_bytes)


def _cast_riders(src_refs, dst_refs):
    for src, dst in zip(src_refs, dst_refs):
        dst[...] = src[...].astype(BF16)


def _slab_rider(w, n_steps, step_of):
    rows, cols = w.shape
    slab = rows // n_steps
    assert slab * n_steps == rows and slab % HEAD_ROWS == 0
    spec = pl.BlockSpec((slab, cols), lambda *ids: (step_of(*ids), 0))
    return spec, spec, jax.ShapeDtypeStruct((rows, cols), BF16)


def _col_rider(w, col0, col1, n_steps, step_of, *, width):
    rows = w.shape[0]
    n_col = (col1 - col0) // width
    assert n_col * width == col1 - col0 and col0 % width == 0 and n_col <= n_steps
    n_row = 1
    while n_row * 2 * n_col <= n_steps and rows % (n_row * 2) == 0:
        n_row *= 2
    rb = rows // n_row
    assert rb % HEAD_ROWS == 0
    n_blk = n_row * n_col

    def blk(*ids):
        return jnp.minimum(step_of(*ids), n_blk - 1)

    in_spec = pl.BlockSpec((rb, width), lambda *ids: (blk(*ids) // n_col, col0 // width + blk(*ids) % n_col))
    out_spec = pl.BlockSpec((rb, width), lambda *ids: (blk(*ids) // n_col, blk(*ids) % n_col))
    return in_spec, out_spec, jax.ShapeDtypeStruct((rows, col1 - col0), BF16)


def _conv_prompt_kernel(xn_ref, wh_ref, wb_ref, wc_ref, cw_ref, *rest, tm, blocks_per_seq, n_riders):
    rider_src = rest[:n_riders]
    y_ref, st_ref = rest[n_riders:n_riders + 2]
    rider_dst = rest[n_riders + 2:2 * n_riders + 2]
    carry_ref, whb_ref, wbb_ref, wcb_ref = rest[2 * n_riders + 2:]
    i = pl.program_id(1)

    @pl.when(i == 0)
    def _():
        whb_ref[...] = wh_ref[...].astype(BF16)
        wbb_ref[...] = wb_ref[...].astype(BF16)
        wcb_ref[...] = wc_ref[...].astype(BF16)

    @pl.when(i % blocks_per_seq == 0)
    def _():
        carry_ref[...] = jnp.zeros(carry_ref.shape, F32)

    xn = xn_ref[...]
    c = _dot(xn, wcb_ref[...])
    h = _dot(xn, whb_ref[...])
    u = c * h
    b = _dot(xn, wbb_ref[...])
    w0 = cw_ref[0:1, :]
    w1 = cw_ref[1:2, :]
    w2 = cw_ref[2:3, :]
    conv = w0 * pltpu.roll(u, 2, 0) + w1 * pltpu.roll(u, 1, 0) + w2 * u
    y_ref[...] = (b * conv).astype(BF16)
    head = jnp.concatenate([carry_ref[...], u[0:HEAD_ROWS]], axis=0)
    conv_head = w0 * pltpu.roll(head, 2, 0) + w1 * pltpu.roll(head, 1, 0) + w2 * head
    y_ref[0:HEAD_ROWS, :] = (b[0:HEAD_ROWS] * conv_head[SUBLANES:]).astype(BF16)
    tail = u[tm - SUBLANES:tm]
    carry_ref[...] = tail
    st_ref[0] = tail
    _cast_riders(rider_src, rider_dst)


def _conv_path_prompt(xn, conv_w, w_in, slab_weights, *, seq, d_conv, tm, tn):
    m, d = xn.shape
    nct = d_conv // tn
    n_i = m // tm
    n_steps = nct * n_i
    step_of = lambda j, i: j * n_i + i
    riders = [_col_rider(w_in, 3 * d_conv, w_in.shape[1], n_steps, step_of, width=tn * 2)]
    riders += [_slab_rider(w, n_steps, step_of) for w in slab_weights]
    kern = functools.partial(_conv_prompt_kernel, tm=tm, blocks_per_seq=seq // tm, n_riders=len(riders))
    return pl.pallas_call(
        kern,
        grid=(nct, n_i),
        in_specs=[
            pl.BlockSpec((tm, d), lambda j, i: (i, 0)),
            pl.BlockSpec((d, tn), lambda j, i: (0, j)),
            pl.BlockSpec((d, tn), lambda j, i: (0, nct + j)),
            pl.BlockSpec((d, tn), lambda j, i: (0, 2 * nct + j)),
            pl.BlockSpec((CONV_WIDTH, tn), lambda j, i: (0, j)),
        ] + [r[0] for r in riders],
        out_specs=[
            pl.BlockSpec((tm, tn), lambda j, i: (i, j)),
            pl.BlockSpec((1, SUBLANES, tn), lambda j, i: (i, 0, j)),
        ] + [r[1] for r in riders],
        out_shape=[
            jax.ShapeDtypeStruct((m, d_conv), BF16),
            jax.ShapeDtypeStruct((n_i, SUBLANES, d_conv), F32),
        ] + [r[2] for r in riders],
        scratch_shapes=[pltpu.VMEM((SUBLANES, tn), F32)] + [pltpu.VMEM((d, tn), BF16)] * 3,
        compiler_params=_params(2),
        name="conv_path_prompt",
    )(xn, w_in, w_in, w_in, conv_w, w_in, *slab_weights)


def _pool_prompt_kernel(x_ref, g_ref, wv_ref, wp_ref, ps_ref, xn_ref, mixed_ref, st_ref, carry_ref, wvb_ref,
                        *, tm, blocks_per_seq, d_group):
    i = pl.program_id(0)
    blk = i % blocks_per_seq

    @pl.when(i == 0)
    def _():
        wvb_ref[...] = wv_ref[...].astype(BF16)

    @pl.when(blk == 0)
    def _():
        carry_ref[...] = jnp.zeros(carry_ref.shape, F32)

    xn = _rmsnorm(x_ref[...], g_ref[...]).astype(BF16)
    xn_ref[...] = xn
    pos = blk * tm + lax.broadcasted_iota(jnp.int32, (tm, 1), 0)
    n_groups = len(POOL_WINDOWS)
    order = sorted(range(n_groups), key=lambda q: (-(q // 2), q))
    v_pairs = {g // 2: _dot(xn, wvb_ref[:, g * d_group:(g + 2) * d_group]) for g in order if g % 2 == 0}
    for g in order:
        w = POOL_WINDOWS[g]
        lo, hi = g * d_group, (g + 1) * d_group
        v = v_pairs[g // 2][:, (g % 2) * d_group:(g % 2 + 1) * d_group]
        s = jnp.concatenate([carry_ref[:, lo:hi], v], axis=0)
        tail = v[tm - POOL_CARRY_ROWS:tm]
        carry_ref[:, lo:hi] = tail
        st_ref[0, :, lo:hi] = tail
        k = 1
        while k < w:
            s = s + pltpu.roll(s, k, 0)
            k *= 2
        cnt = jnp.minimum(pos + 1, w).astype(F32)
        pooled = s[POOL_CARRY_ROWS:] / cnt - v
        mixed = _dot(pooled.astype(BF16), wp_ref[g].astype(BF16)) * ps_ref[:, lo:hi]
        mixed_ref[:, lo:hi] = mixed.astype(BF16)


def _pool_path_prompt(x2d, g, w_in, w_pool, pool_scale, *, n_batch, seq, d_conv, d_pool, tm):
    m, d = x2d.shape
    n_groups = len(POOL_WINDOWS)
    d_group = d_pool // n_groups
    v_off = 3 * d_conv
    kern = functools.partial(_pool_prompt_kernel, tm=tm, blocks_per_seq=seq // tm, d_group=d_group)
    return pl.pallas_call(
        kern,
        grid=(m // tm,),
        in_specs=[
            pl.BlockSpec((tm, d), lambda i: (i, 0)),
            pl.BlockSpec((1, d), lambda i: (0, 0)),
            pl.BlockSpec((d, d_pool), lambda i: (0, v_off // d_pool)),
            pl.BlockSpec((n_groups, d_group, d_group), lambda i: (0, 0, 0)),
            pl.BlockSpec((1, d_pool), lambda i: (0, 0)),
        ],
        out_specs=[
            pl.BlockSpec((tm, d), lambda i: (i, 0)),
            pl.BlockSpec((tm, d_pool), lambda i: (i, 0)),
            pl.BlockSpec((1, POOL_CARRY_ROWS, d_pool), lambda i: ((i * tm) // seq, 0, 0)),
        ],
        out_shape=[
            jax.ShapeDtypeStruct((m, d), BF16),
            jax.ShapeDtypeStruct((m, d_pool), BF16),
            jax.ShapeDtypeStruct((n_batch, POOL_CARRY_ROWS, d_pool), F32),
        ],
        scratch_shapes=[pltpu.VMEM((POOL_CARRY_ROWS, d_pool), F32), pltpu.VMEM((d, d_pool), BF16)],
        compiler_params=_params(1),
        name="pool_path_prompt",
    )(x2d, g, w_in, w_pool, pool_scale)


def _sample_mixers_kernel(x_ref, g_ref, wh_ref, wb_ref, wc_ref, cw_ref, sc_ref, wv_ref, wp_ref, ps_ref, sp_ref,
                          xn_ref, y_ref, stc_ref, mixed_ref, stp_ref, *, n_seq, n_t, d_group, n_pool):
    j = pl.program_id(0)

    @pl.when(j == 0)
    def _():
        xn_ref[...] = _rmsnorm(x_ref[...], g_ref[...]).astype(BF16)

    @pl.when(j >= n_pool)
    def _():
        xn = xn_ref[...]
        c = _dot(xn, wc_ref[...].astype(BF16))
        h = _dot(xn, wh_ref[...].astype(BF16))
        u = c * h
        b = _dot(xn, wb_ref[...].astype(BF16))
        w0 = cw_ref[0:1, :]
        w1 = cw_ref[1:2, :]
        w2 = cw_ref[2:3, :]
        n_state = CONV_WIDTH - 1
        ext = [sc_ref[k * n_seq:(k + 1) * n_seq, :] for k in range(n_state)]
        ext += [u[t * n_seq:(t + 1) * n_seq] for t in range(n_t)]
        for t in range(n_t):
            conv = w0 * ext[t] + w1 * ext[t + 1] + w2 * ext[t + 2]
            y_ref[t * n_seq:(t + 1) * n_seq, :] = (b[t * n_seq:(t + 1) * n_seq] * conv).astype(BF16)
        for k in range(n_state):
            stc_ref[k * n_seq:(k + 1) * n_seq, :] = ext[n_t + k]

    @pl.when(j < n_pool)
    def _():
        p = j
        n_steps = len(POOL_WINDOWS) // POOL_PAIR
        v = _dot(xn_ref[...], wv_ref[...])
        for l in range(POOL_PAIR):
            lo, hi = l * d_group, (l + 1) * d_group
            windows = [POOL_WINDOWS[q * POOL_PAIR + l] for q in range(n_steps)]
            ext = [sp_ref[k * n_seq:(k + 1) * n_seq, lo:hi] for k in range(POOL_BUF)]
            ext += [v[t * n_seq:(t + 1) * n_seq, lo:hi] for t in range(n_t)]
            wp = wp_ref[l].astype(BF16)
            for t in range(n_t):
                s = ext[POOL_BUF + t]
                means = {}
                for k in range(1, max(windows) + 1):
                    if k in windows:
                        means[k] = s / float(min(PAST_LEN + 1, k))
                    if k < max(windows):
                        s = s + ext[POOL_BUF + t - k]
                mean = means[windows[-1]]
                for q in range(n_steps - 2, -1, -1):
                    mean = jnp.where(p == q, means[windows[q]], mean)
                pooled = mean - ext[POOL_BUF + t]
                mixed = _dot(pooled.astype(BF16), wp) * ps_ref[:, lo:hi]
                mixed_ref[t * n_seq:(t + 1) * n_seq, lo:hi] = mixed.astype(BF16)
            for k in range(POOL_BUF):
                stp_ref[k * n_seq:(k + 1) * n_seq, lo:hi] = ext[n_t + k]


def _sample_mixers(x2d, g, w_in, conv_w, conv_state_tm, w_tail_bf, w_pool, pool_scale, pool_state_tm,
                   *, n_seq, n_t, d_conv, d_pool, tn):
    m, d = x2d.shape
    nct = d_conv // tn
    n_groups = len(POOL_WINDOWS)
    d_group = d_pool // n_groups
    pw = POOL_PAIR * d_group
    n_state = CONV_WIDTH - 1
    n_pool = n_groups // POOL_PAIR
    pj = lambda j: jnp.minimum(j, n_pool - 1)
    cj = lambda j: jnp.maximum(j - n_pool, 0)
    kern = functools.partial(_sample_mixers_kernel, n_seq=n_seq, n_t=n_t, d_group=d_group, n_pool=n_pool)
    return pl.pallas_call(
        kern,
        grid=(n_pool + nct,),
        in_specs=[
            pl.BlockSpec((m, d), lambda j: (0, 0)),
            pl.BlockSpec((1, d), lambda j: (0, 0)),
            pl.BlockSpec((d, tn), lambda j: (0, cj(j))),
            pl.BlockSpec((d, tn), lambda j: (0, nct + cj(j))),
            pl.BlockSpec((d, tn), lambda j: (0, 2 * nct + cj(j))),
            pl.BlockSpec((CONV_WIDTH, tn), lambda j: (0, cj(j))),
            pl.BlockSpec((n_state * n_seq, tn), lambda j: (0, cj(j))),
            pl.BlockSpec((d, pw), lambda j: (0, pj(j))),
            pl.BlockSpec((POOL_PAIR, d_group, d_group), lambda j: (pj(j), 0, 0)),
            pl.BlockSpec((1, pw), lambda j: (0, pj(j))),
            pl.BlockSpec((POOL_BUF * n_seq, pw), lambda j: (0, pj(j))),
        ],
        out_specs=[
            pl.BlockSpec((m, d), lambda j: (0, 0)),
            pl.BlockSpec((m, tn), lambda j: (0, cj(j))),
            pl.BlockSpec((n_state * n_seq, tn), lambda j: (0, cj(j))),
            pl.BlockSpec((m, pw), lambda j: (0, pj(j))),
            pl.BlockSpec((POOL_BUF * n_seq, pw), lambda j: (0, pj(j))),
        ],
        out_shape=[
            jax.ShapeDtypeStruct((m, d), BF16),
            jax.ShapeDtypeStruct((m, d_conv), BF16),
            jax.ShapeDtypeStruct((n_state * n_seq, d_conv), F32),
            jax.ShapeDtypeStruct((m, d_pool), BF16),
            jax.ShapeDtypeStruct((POOL_BUF * n_seq, d_pool), F32),
        ],
        compiler_params=_params(1, HIGH_VMEM_LIMIT_BYTES),
        name="sample_mixers",
    )(x2d, g, w_in, w_in, w_in, conv_w, conv_state_tm, w_tail_bf, w_pool, pool_scale, pool_state_tm)


def _gate_merge_kernel(xn_ref, yc_ref, mx_ref, wgc_ref, wgp_ref, wbc_ref, wbp_ref, *rest, n_riders):
    rider_src = rest[:n_riders]
    o_ref = rest[n_riders]
    rider_dst = rest[n_riders + 1:2 * n_riders + 1]
    xn = xn_ref[...]
    gc = _dot(xn, wgc_ref[...])
    gp = _dot(xn, wgp_ref[...])
    y_conv = _dot(yc_ref[...], wbc_ref[...])
    y_pool = _dot(mx_ref[...], wbp_ref[...])
    o_ref[...] = (jax.nn.sigmoid(gc) * y_conv + jax.nn.sigmoid(gp) * y_pool).astype(BF16)
    _cast_riders(rider_src, rider_dst)


def _gate_merge(xn, ycin, mixed, w_tail_bf, w_brc_bf, w_brp_bf, slab_weights, *, gc_off, gp_off, tm, tn):
    m, d = xn.shape
    d_conv = ycin.shape[1]
    d_pool = mixed.shape[1]
    nct = d // tn
    riders = [_slab_rider(w, (m // tm) * nct, lambda i, j: i * nct + j) for w in slab_weights]
    out = pl.pallas_call(
        functools.partial(_gate_merge_kernel, n_riders=len(riders)),
        grid=(m // tm, nct),
        in_specs=[
            pl.BlockSpec((tm, d), lambda i, j: (i, 0)),
            pl.BlockSpec((tm, d_conv), lambda i, j: (i, 0)),
            pl.BlockSpec((tm, d_pool), lambda i, j: (i, 0)),
            pl.BlockSpec((d, tn), lambda i, j: (0, gc_off // tn + j)),
            pl.BlockSpec((d, tn), lambda i, j: (0, gp_off // tn + j)),
            pl.BlockSpec((d_conv, tn), lambda i, j: (0, j)),
            pl.BlockSpec((d_pool, tn), lambda i, j: (0, j)),
        ] + [r[0] for r in riders],
        out_specs=[pl.BlockSpec((tm, tn), lambda i, j: (i, j))] + [r[1] for r in riders],
        out_shape=[jax.ShapeDtypeStruct((m, d), BF16)] + [r[2] for r in riders],
        compiler_params=_params(2),
        name="gate_merge",
    )(xn, ycin, mixed, w_tail_bf, w_tail_bf, w_brc_bf, w_brp_bf, *slab_weights)
    return out[0], out[1:]


def _out_proj_kernel(xp_ref, mp_ref, xs_ref, ms_ref, wo_ref, x1p_ref, x1s_ref, *, n_prompt):
    i = pl.program_id(0)

    @pl.when(i < n_prompt)
    def _():
        x1p_ref[...] = xp_ref[...] + _dot(mp_ref[...], wo_ref[...])

    @pl.when(i == n_prompt)
    def _():
        x1s_ref[...] = xs_ref[...] + _dot(ms_ref[...], wo_ref[...])


def _out_proj(xp, merged_p, xs, merged_s, w_o_bf, *, tm):
    m, d = xp.shape
    n_prompt = m // tm
    assert xs.shape[0] == tm
    last = n_prompt - 1
    row = lambda i: (jnp.minimum(i, last), 0)
    const = lambda i: (0, 0)
    return pl.pallas_call(
        functools.partial(_out_proj_kernel, n_prompt=n_prompt),
        grid=(n_prompt + 1,),
        in_specs=[
            pl.BlockSpec((tm, d), row),
            pl.BlockSpec((tm, d), row),
            pl.BlockSpec((tm, d), const),
            pl.BlockSpec((tm, d), const),
            pl.BlockSpec((d, d), const, pipeline_mode=pl.Buffered(1)),
        ],
        out_specs=[pl.BlockSpec((tm, d), row), pl.BlockSpec((tm, d), const)],
        out_shape=[jax.ShapeDtypeStruct((m, d), F32), jax.ShapeDtypeStruct((tm, d), F32)],
        compiler_params=_params(1),
        name="out_proj_residual",
    )(xp, merged_p, xs, merged_s, w_o_bf)


def _ffn_kernel(xp_ref, xs_ref, wg_ref, wu_ref, wd_ref, gf_ref, gl_ref, op_ref, os_ref, hn_ref, *, final_norm):
    f = pl.program_id(1)
    n_f = pl.num_programs(1)
    tm = xp_ref.shape[0]

    def down_proj():
        hn = hn_ref[...]
        half = wg_ref.shape[1] // 2
        ff = []
        for k in range(2):
            cols = slice(k * half, (k + 1) * half)
            gate = _dot(hn, wg_ref[:, cols])
            up = _dot(hn, wu_ref[:, cols])
            ff.append(((gate * jax.nn.sigmoid(gate)) * up).astype(BF16))
        return _dot(jnp.concatenate(ff, axis=1), wd_ref[...])

    @pl.when(f == 0)
    def _():
        hn_ref[0:tm, :] = _rmsnorm(xp_ref[...], gf_ref[...]).astype(BF16)
        hn_ref[tm:, :] = _rmsnorm(xs_ref[...], gf_ref[...]).astype(BF16)
        down = down_proj()
        op_ref[...] = down[0:tm]
        os_ref[...] = down[tm:]

    @pl.when(jnp.logical_and(f > 0, f < n_f - 1))
    def _():
        down = down_proj()
        op_ref[...] += down[0:tm]
        os_ref[...] += down[tm:]

    @pl.when(f == n_f - 1)
    def _():
        down = down_proj()
        yp = op_ref[...] + down[0:tm] + xp_ref[...]
        ys = os_ref[...] + down[tm:] + xs_ref[...]
        op_ref[...] = _rmsnorm(yp, gl_ref[...]) if final_norm else yp
        os_ref[...] = _rmsnorm(ys, gl_ref[...]) if final_norm else ys


def _ffn(x1p, x1s, w_gate_bf, w_up_bf, w_down_bf, g_ffn, g_final, *, final_norm, tm, tf):
    m, d = x1p.shape
    d_ff = w_gate_bf.shape[1]
    n_blocks = m // tm
    rs = x1s.shape[0] // n_blocks
    assert rs * n_blocks == x1s.shape[0] and rs % HEAD_ROWS == 0
    assert d_ff // tf >= 2
    kern = functools.partial(_ffn_kernel, final_norm=final_norm)
    return pl.pallas_call(
        kern,
        grid=(n_blocks, d_ff // tf),
        in_specs=[
            pl.BlockSpec((tm, d), lambda i, f: (i, 0)),
            pl.BlockSpec((rs, d), lambda i, f: (i, 0)),
            pl.BlockSpec((d, tf), lambda i, f: (0, f)),
            pl.BlockSpec((d, tf), lambda i, f: (0, f)),
            pl.BlockSpec((tf, d), lambda i, f: (f, 0)),
            pl.BlockSpec((1, d), lambda i, f: (0, 0)),
            pl.BlockSpec((1, d), lambda i, f: (0, 0)),
        ],
        out_specs=[pl.BlockSpec((tm, d), lambda i, f: (i, 0)), pl.BlockSpec((rs, d), lambda i, f: (i, 0))],
        out_shape=[jax.ShapeDtypeStruct((m, d), F32), jax.ShapeDtypeStruct(x1s.shape, F32)],
        scratch_shapes=[pltpu.VMEM((tm + rs, d), BF16)],
        compiler_params=_params(2, HIGH_VMEM_LIMIT_BYTES),
        name="swiglu_ffn",
    )(x1p, x1s, w_gate_bf, w_up_bf, w_down_bf, g_ffn, g_final)


def _tiles(d_conv, d_ff):
    tm = 1024
    tm_pool = 1024
    tn_conv = 512 if d_conv % 512 == 0 else 256
    tn_merge = 512
    tf = 512 if d_ff % 512 == 0 else 256
    return tm, tm_pool, tn_conv, tn_merge, tf


def kernel(x_prompt, x_sample, state_conv, state_pool, norm_mix, w_in, conv_w, w_pool, pool_scale, w_br_conv,
           w_br_pool, w_o, norm_ffn, w_gate, w_up, w_down, norm_final):
    n_batch, seq, d = x_prompt.shape
    n_seq, n_t, _ = x_sample.shape
    depth = w_in.shape[0]
    d_conv = conv_w.shape[2]
    d_pool = pool_scale.shape[1]
    d_ff = w_gate.shape[2]
    gc_off = d_pool
    gp_off = gc_off + d
    tm, tm_pool, tn_conv, tn_merge, tf = _tiles(d_conv, d_ff)
    tm_out = n_t * n_seq
    assert seq % tm == 0 and seq % tm_pool == 0 and tm_pool >= POOL_CARRY_ROWS and n_t >= CONV_WIDTH - 1
    assert (n_batch * seq) % tm_out == 0 and tm_out % HEAD_ROWS == 0
    assert n_seq % 16 == 0 and gc_off % tn_merge == 0 and gp_off % tn_merge == 0

    yp = x_prompt.reshape(n_batch * seq, d)
    ys = jnp.transpose(x_sample, (1, 0, 2)).reshape(n_t * n_seq, d)
    g_final = norm_final.reshape(1, d)
    conv_p, pool_p, conv_s, pool_s = [], [], [], []
    for l in range(depth):
        g_mix = norm_mix[l].reshape(1, d)
        g_ffn = norm_ffn[l].reshape(1, d)
        ps = pool_scale[l].reshape(1, d_pool)
        last = l == depth - 1

        xn, mixed, st_p = _pool_path_prompt(yp, g_mix, w_in[l], w_pool[l], ps, n_batch=n_batch, seq=seq,
                                            d_conv=d_conv, d_pool=d_pool, tm=tm_pool)
        ycin, st_c, w_tail_bf, w_brc_bf, w_brp_bf = _conv_path_prompt(
            xn, conv_w[l], w_in[l], (w_br_conv[l], w_br_pool[l]), seq=seq, d_conv=d_conv, tm=tm, tn=tn_conv)
        merged, (w_o_bf, w_down_bf, w_gate_bf, w_up_bf) = _gate_merge(
            xn, ycin, mixed, w_tail_bf, w_brc_bf, w_brp_bf, (w_o[l], w_down[l], w_gate[l], w_up[l]),
            gc_off=gc_off, gp_off=gp_off, tm=tm, tn=tn_merge)
        bps = seq // tm
        conv_p.append(st_c[bps - 1::bps, SUBLANES - (CONV_WIDTH - 1):, :])
        pool_p.append(st_p[:, POOL_CARRY_ROWS - POOL_BUF:, :])

        sc_tm = jnp.transpose(state_conv[l], (1, 0, 2)).reshape((CONV_WIDTH - 1) * n_seq, d_conv)
        sp_tm = jnp.transpose(state_pool[l], (1, 0, 2)).reshape(POOL_BUF * n_seq, d_pool)
        xn_s, ycin_s, st_c, mixed_s, st_p = _sample_mixers(ys, g_mix, w_in[l], conv_w[l], sc_tm, w_tail_bf, w_pool[l],
                                                            ps, sp_tm, n_seq=n_seq, n_t=n_t, d_conv=d_conv,
                                                            d_pool=d_pool, tn=tn_conv)
        merged_s, _ = _gate_merge(xn_s, ycin_s, mixed_s, w_tail_bf, w_brc_bf, w_brp_bf, (),
                                  gc_off=gc_off, gp_off=gp_off, tm=n_t * n_seq, tn=tn_merge)
        conv_s.append(jnp.transpose(st_c.reshape(CONV_WIDTH - 1, n_seq, d_conv), (1, 0, 2)))
        pool_s.append(jnp.transpose(st_p.reshape(POOL_BUF, n_seq, d_pool), (1, 0, 2)))

        x1p, x1s = _out_proj(yp, merged, ys, merged_s, w_o_bf, tm=tm_out)
        yp, ys = _ffn(x1p, x1s, w_gate_bf, w_up_bf, w_down_bf, g_ffn, g_final, final_norm=last, tm=tm, tf=tf)

    y_prompt = yp.reshape(n_batch, seq, d)
    y_sample = jnp.transpose(ys.reshape(n_t, n_seq, d), (1, 0, 2))
    return (y_prompt, y_sample, jnp.stack(conv_p, axis=0), jnp.stack(pool_p, axis=0),
            jnp.stack(conv_s, axis=0), jnp.stack(pool_s, axis=0))
```
